```python
import jax, jax.numpy as jnp
from jax import lax
import numpy as np

D_MODEL = 2048
BATCH = 4
SEQ = 2048
DEPTH = 1
DEC_BATCH = 128
DEC_SEQ = 8
PAST_LEN = 2048
PAGE_SIZE = 128

HEAD_DIM = 128
N_HEADS = D_MODEL // HEAD_DIM
H_FOX = N_HEADS // 2
H_SB = N_HEADS - H_FOX
W_FOX = H_FOX * HEAD_DIM
W_SB = H_SB * HEAD_DIM
D_MIX = W_FOX + W_SB
D_PROJ = 3 * W_FOX + 3 * W_SB + H_FOX
D_FF = 4 * D_MODEL
Q_BLOCK = 128
EPS = 1e-6
FORGET_BIAS_MEAN = 3.0

kernel_name = "hymba_fox_stickbreaking_decoder_step"


def _rmsnorm(x, g):
    xf = x.astype(jnp.float32)
    y = xf * lax.rsqrt(jnp.mean(xf * xf, axis=-1, keepdims=True) + EPS)
    return (y * g.astype(jnp.float32)).astype(x.dtype)


def _segment_scores(q, ks):
    return jnp.concatenate(
        [jnp.einsum('bqhd,bkhd->bhqk', q, k, preferred_element_type=jnp.float32) for k in ks],
        axis=-1)


def _segment_apply(w, vs):
    out, off = None, 0
    for v in vs:
        n = v.shape[1]
        term = jnp.einsum('bhqk,bkhd->bqhd', w[..., off:off + n].astype(v.dtype), v,
                          preferred_element_type=jnp.float32)
        out = term if out is None else out + term
        off += n
    return out


def _sweep(block_fn, q_pos, q_arrays):
    sq = q_pos.shape[0]
    qb = min(Q_BLOCK, sq)
    nb = -(-sq // qb)
    pad = nb * qb - sq

    def split(a):
        a = jnp.pad(a, [(0, 0), (0, pad)] + [(0, 0)] * (a.ndim - 2))
        return jnp.moveaxis(a.reshape(a.shape[0], nb, qb, *a.shape[2:]), 1, 0)

    pos_blocks = jnp.pad(q_pos, (0, pad), mode='edge').reshape(nb, qb)
    out = lax.map(lambda args: block_fn(*args), (pos_blocks,) + tuple(split(a) for a in q_arrays))
    out = jnp.moveaxis(out, 0, 1)
    out = out.reshape(out.shape[0], nb * qb, *out.shape[3:])
    return out[:, :sq]


def _fox_attention(q, ks, vs, cq, ck, q_pos, k_pos):
    scale = HEAD_DIM ** -0.5
    ck_t = jnp.swapaxes(ck, 1, 2)

    def block(qpos, qb, cqb):
        s = _segment_scores(qb, ks) * scale
        s = s + jnp.swapaxes(cqb, 1, 2)[..., :, None] - ck_t[..., None, :]
        s = jnp.where(k_pos[None, :] <= qpos[:, None], s, -jnp.inf)
        p = jax.nn.softmax(s, axis=-1)
        return _segment_apply(p, vs)

    return _sweep(block, q_pos, (q, cq))


def _stick_breaking_attention(q, ks, vs, q_pos, k_pos):
    scale = HEAD_DIM ** -0.5

    def block(qpos, qb):
        z = _segment_scores(qb, ks) * scale
        strict = k_pos[None, :] < qpos[:, None]
        log_beta = jax.nn.log_sigmoid(z)
        log_1m = jnp.where(strict, jax.nn.log_sigmoid(-z), 0.0)
        suffix = lax.cumsum(log_1m, axis=log_1m.ndim - 1, reverse=True) - log_1m
        a = jnp.where(strict, jnp.exp(log_beta + suffix), 0.0)
        return _segment_apply(a, vs)

    return _sweep(block, q_pos, (q,))


def _hybrid_layer(x, q_pos, k_pos, past, g_mix, w_in, b_f, g_out_fox, g_out_sb,
                  w_out, g_ffn, w_up, w_down):
    b, s, _ = x.shape
    xn = _rmsnorm(x, g_mix)
    proj = xn @ w_in
    cuts = np.cumsum([W_FOX, W_FOX, W_FOX, W_SB, W_SB, W_SB]).tolist()
    q_f, k_f, v_f, q_s, k_s, v_s, gate = jnp.split(proj, cuts, axis=-1)
    q_f = q_f.reshape(b, s, H_FOX, HEAD_DIM)
    k_f = k_f.reshape(b, s, H_FOX, HEAD_DIM)
    v_f = v_f.reshape(b, s, H_FOX, HEAD_DIM)
    q_s = q_s.reshape(b, s, H_SB, HEAD_DIM)
    k_s = k_s.reshape(b, s, H_SB, HEAD_DIM)
    v_s = v_s.reshape(b, s, H_SB, HEAD_DIM)
    logf = jax.nn.log_sigmoid(gate.astype(jnp.float32) + b_f.astype(jnp.float32))

    if past is None:
        kf_seg, vf_seg, ks_seg, vs_seg = (k_f,), (v_f,), (k_s,), (v_s,)
        logf_all = logf
    else:
        p_kf, p_vf, p_logf, p_ks, p_vs = past
        kf_seg, vf_seg, ks_seg, vs_seg = (p_kf, k_f), (p_vf, v_f), (p_ks, k_s), (p_vs, v_s)
        logf_all = jnp.concatenate([p_logf.astype(jnp.float32), logf], axis=1)
    c_all = jnp.cumsum(logf_all, axis=1)
    cq = c_all[:, -s:]

    o_f = _fox_attention(q_f, kf_seg, vf_seg, cq, c_all, q_pos, k_pos)
    o_s = _stick_breaking_attention(q_s, ks_seg, vs_seg, q_pos, k_pos)
    o = jnp.concatenate([_rmsnorm(o_f.reshape(b, s, W_FOX), g_out_fox),
                         _rmsnorm(o_s.reshape(b, s, W_SB), g_out_sb)], axis=-1).astype(x.dtype)
    h = x + o @ w_out
    u = jax.nn.relu(_rmsnorm(h, g_ffn) @ w_up)
    h = h + (u * u) @ w_down
    return h, (k_f, v_f, logf, k_s, v_s)


def setup_inputs(seed: int = 0) -> dict:
    key = jax.random.key(seed)
    ks = jax.random.split(key, 24)
    f32 = jnp.float32
    n_pages = PAST_LEN // PAGE_SIZE
    n_used = DEC_BATCH * n_pages
    n_pool = n_used + max(1, n_used // 4)
    nrm = lambda k, shape, sc=1.0: jax.random.normal(k, shape, f32) * sc
    page_table = jax.random.permutation(ks[0], n_pool)[:n_used].reshape(DEC_BATCH, n_pages).astype(jnp.int32)
    return {
        "x_prompt": nrm(ks[1], (BATCH, SEQ, D_MODEL)),
        "x_sample": nrm(ks[2], (DEC_BATCH, DEC_SEQ, D_MODEL)),
        "cache_fox_k": nrm(ks[3], (DEPTH, n_pool, PAGE_SIZE, H_FOX, HEAD_DIM)),
        "cache_fox_v": nrm(ks[4], (DEPTH, n_pool, PAGE_SIZE, H_FOX, HEAD_DIM)),
        "cache_fox_logf": jax.nn.log_sigmoid(FORGET_BIAS_MEAN + nrm(ks[5], (DEPTH, n_pool, PAGE_SIZE, H_FOX))),
        "cache_sb_k": nrm(ks[6], (DEPTH, n_pool, PAGE_SIZE, H_SB, HEAD_DIM)),
        "cache_sb_v": nrm(ks[7], (DEPTH, n_pool, PAGE_SIZE, H_SB, HEAD_DIM)),
        "page_table": page_table,
        "g_mix": 1.0 + nrm(ks[8], (DEPTH, D_MODEL), 0.02),
        "w_in": nrm(ks[9], (DEPTH, D_MODEL, D_PROJ), D_MODEL ** -0.5),
        "b_f": FORGET_BIAS_MEAN + nrm(ks[10], (DEPTH, H_FOX), 0.1),
        "g_out_fox": 1.0 + nrm(ks[11], (DEPTH, W_FOX), 0.02),
        "g_out_sb": 1.0 + nrm(ks[12], (DEPTH, W_SB), 0.02),
        "w_out": nrm(ks[13], (DEPTH, D_MIX, D_MODEL), D_MIX ** -0.5),
        "g_ffn": 1.0 + nrm(ks[14], (DEPTH, D_MODEL), 0.02),
        "w_up": nrm(ks[15], (DEPTH, D_MODEL, D_FF), D_MODEL ** -0.5),
        "w_down": nrm(ks[16], (DEPTH, D_FF, D_MODEL), D_FF ** -0.5),
        "g_final": 1.0 + nrm(ks[17], (D_MODEL,), 0.02),
    }


def reference(x_prompt, x_sample, cache_fox_k, cache_fox_v, cache_fox_logf, cache_sb_k, cache_sb_v,
              page_table, g_mix, w_in, b_f, g_out_fox, g_out_sb, w_out, g_ffn, w_up, w_down, g_final):
    seq = x_prompt.shape[1]
    dec_seq = x_sample.shape[1]
    past_len = page_table.shape[1] * cache_fox_k.shape[2]
    p_pos = jnp.arange(seq, dtype=jnp.int32)
    s_qpos = past_len + jnp.arange(dec_seq, dtype=jnp.int32)
    s_kpos = jnp.arange(past_len + dec_seq, dtype=jnp.int32)

    def gather(c, l):
        g = c[l][page_table]
        return g.reshape(g.shape[0], -1, *g.shape[3:])

    hp, hs = x_prompt, x_sample
    p_rows = [[] for _ in range(5)]
    s_rows = [[] for _ in range(5)]
    for l in range(DEPTH):
        w = (g_mix[l], w_in[l], b_f[l], g_out_fox[l], g_out_sb[l], w_out[l], g_ffn[l], w_up[l], w_down[l])
        hp, new_p = _hybrid_layer(hp, p_pos, p_pos, None, *w)
        past = (gather(cache_fox_k, l), gather(cache_fox_v, l), gather(cache_fox_logf, l),
                gather(cache_sb_k, l), gather(cache_sb_v, l))
        hs, new_s = _hybrid_layer(hs, s_qpos, s_kpos, past, *w)
        for i in range(5):
            p_rows[i].append(new_p[i])
            s_rows[i].append(new_s[i])

    y_prompt = _rmsnorm(hp, g_final)
    y_sample = _rmsnorm(hs, g_final)
    p_fk, p_fv, p_fl, p_sk, p_sv = [jnp.stack(r, axis=0) for r in p_rows]
    s_fk, s_fv, s_fl, s_sk, s_sv = [jnp.stack(r, axis=0) for r in s_rows]
    return (y_prompt, y_sample, p_fk, p_fv, p_fl, p_sk, p_sv, s_fk, s_fv, s_fl, s_sk, s_sv)
```

```python
import functools

import jax
import jax.numpy as jnp
from jax import lax
from jax.experimental import pallas as pl
from jax.experimental.pallas import tpu as pltpu

F32 = jnp.float32
BF16 = jnp.bfloat16

EPS = 1e-6
HEAD_DIM = 128
N_HEAD = 8
W_MIX = N_HEAD * HEAD_DIM
SCALE = HEAD_DIM ** -0.5
LANES = 128
V7X_VMEM_BYTES = 64 * 1024 * 1024
COMPILER_TEMP_BYTES = 12 * 1024 * 1024

NT_DIMS = (((1,), (1,)), ((), ()))


def _vmem_limit(block_bytes, scratch_bytes=0):
    need = 2 * block_bytes + scratch_bytes + COMPILER_TEMP_BYTES
    return int(min(need, V7X_VMEM_BYTES - 4 * 1024 * 1024))


def _nbytes(shape, dtype):
    n = 1
    for s in shape:
        n *= s
    return n * jnp.dtype(dtype).itemsize


def _params(block_bytes, scratch_bytes=0, semantics=None):
    return pltpu.CompilerParams(
        dimension_semantics=semantics,
        vmem_limit_bytes=_vmem_limit(block_bytes, scratch_bytes))


def _rms(x, g):
    ms = jnp.mean(x * x, axis=-1, keepdims=True)
    return x * lax.rsqrt(ms + EPS) * g


def _log_sigmoid_pair(z):
    t = jnp.log1p(jnp.exp(-jnp.abs(z)))
    return jnp.minimum(z, 0.0) - t, -jnp.maximum(z, 0.0) - t


def _tri_rhs(copies, suffix):
    r = lax.broadcasted_iota(jnp.int32, (copies * LANES, LANES), 0) & (LANES - 1)
    c = lax.broadcasted_iota(jnp.int32, (copies * LANES, LANES), 1)
    cond = (r > c) if suffix else (r <= c)
    return jnp.where(cond, 1.0, 0.0).astype(BF16)


def _split3(x):
    hi = x.astype(BF16)
    r = x - hi.astype(F32)
    mid = r.astype(BF16)
    lo = (r - mid.astype(F32)).astype(BF16)
    return hi, mid, lo


def _prefix_lanes(x, tri3):
    hi, mid, lo = _split3(x)
    return jnp.dot(jnp.concatenate([hi, mid, lo], axis=1), tri3,
                   preferred_element_type=F32)


def _suffix_lanes(x, tri2):
    hi = x.astype(BF16)
    lo = (x - hi.astype(F32)).astype(BF16)
    return jnp.dot(jnp.concatenate([hi, lo], axis=1), tri2,
                   preferred_element_type=F32)


def _norm_kernel(x_ref, g_ref, o_ref):
    o_ref[...] = _rms(x_ref[...], g_ref[...]).astype(o_ref.dtype)


def _norm_rows(x, g, tm=512):
    m, d = x.shape
    blk = _nbytes((tm, d), F32) + _nbytes((tm, d), BF16)
    return pl.pallas_call(
        _norm_kernel,
        grid=(m // tm,),
        in_specs=[pl.BlockSpec((tm, d), lambda i: (i, 0)),
                  pl.BlockSpec((1, d), lambda i: (0, 0))],
        out_specs=pl.BlockSpec((tm, d), lambda i: (i, 0)),
        out_shape=jax.ShapeDtypeStruct((m, d), BF16),
        compiler_params=_params(blk, semantics=("arbitrary",)),
        name="norm_rows",
    )(x, g.reshape(1, d))


def _proj_kernel(x_ref, w_ref, o_ref):
    o_ref[...] = jnp.dot(x_ref[...], w_ref[...], preferred_element_type=F32)


def _proj_segment(xn, w_bf, seg, tm=512):
    m, d = xn.shape
    blk = _nbytes((tm, d), BF16) + _nbytes((d, W_MIX), BF16) + _nbytes((tm, W_MIX), F32)
    return pl.pallas_call(
        _proj_kernel,
        grid=(m // tm,),
        in_specs=[pl.BlockSpec((tm, d), lambda i: (i, 0)),
                  pl.BlockSpec((d, W_MIX), lambda i: (0, seg))],
        out_specs=pl.BlockSpec((tm, W_MIX), lambda i: (i, 0)),
        out_shape=jax.ShapeDtypeStruct((m, W_MIX), F32),
        compiler_params=_params(blk, semantics=("arbitrary",)),
        name=f"proj_seg{seg}",
    )(xn, w_bf)


def _gate_kernel(x_ref, w_ref, b_ref, o_ref):
    y = jnp.dot(x_ref[...], w_ref[...], preferred_element_type=F32) + b_ref[...]
    o_ref[...] = _log_sigmoid_pair(y)[0]


def _gate_logf(xn, wg_bf, bg, tm=512):
    m, d = xn.shape
    blk = _nbytes((tm, d), BF16) + _nbytes((d, LANES), BF16) + _nbytes((tm, LANES), F32)
    return pl.pallas_call(
        _gate_kernel,
        grid=(m // tm,),
        in_specs=[pl.BlockSpec((tm, d), lambda i: (i, 0)),
                  pl.BlockSpec((d, LANES), lambda i: (0, 0)),
                  pl.BlockSpec((1, LANES), lambda i: (0, 0))],
        out_specs=pl.BlockSpec((tm, LANES), lambda i: (i, 0)),
        out_shape=jax.ShapeDtypeStruct((m, LANES), F32),
        compiler_params=_params(blk, semantics=("arbitrary",)),
        name="gate_logf",
    )(xn, wg_bf, bg)


def _attn_out_kernel(of_ref, os_ref, gf_ref, gs_ref, x_ref, w_ref, gn_ref, h_ref, hn_ref):
    nf = _rms(of_ref[...], gf_ref[...]).astype(BF16)
    ns = _rms(os_ref[...], gs_ref[...]).astype(BF16)
    proj = (jnp.dot(nf, w_ref[0:W_MIX, :], preferred_element_type=F32)
            + jnp.dot(ns, w_ref[W_MIX:2 * W_MIX, :], preferred_element_type=F32))
    h = x_ref[...] + proj
    h_ref[...] = h
    hn_ref[...] = _rms(h, gn_ref[...]).astype(BF16)


def _attn_out(o_f, o_s, g_f, g_s, x, w_out_bf, g_ffn, tm=256):
    m, d = x.shape
    blk = (2 * _nbytes((tm, W_MIX), F32) + 2 * _nbytes((tm, d), F32)
           + _nbytes((2 * W_MIX, d), BF16) + _nbytes((tm, d), BF16))
    row = lambda i: (i, 0)
    const = lambda i: (0, 0)
    return pl.pallas_call(
        _attn_out_kernel,
        grid=(m // tm,),
        in_specs=[pl.BlockSpec((tm, W_MIX), row), pl.BlockSpec((tm, W_MIX), row),
                  pl.BlockSpec((1, W_MIX), const), pl.BlockSpec((1, W_MIX), const),
                  pl.BlockSpec((tm, d), row),
                  pl.BlockSpec((2 * W_MIX, d), const),
                  pl.BlockSpec((1, d), const)],
        out_specs=[pl.BlockSpec((tm, d), row), pl.BlockSpec((tm, d), row)],
        out_shape=[jax.ShapeDtypeStruct((m, d), F32), jax.ShapeDtypeStruct((m, d), BF16)],
        compiler_params=_params(blk, semantics=("arbitrary",)),
        name="attn_out",
    )(o_f, o_s, g_f.reshape(1, -1), g_s.reshape(1, -1), x, w_out_bf, g_ffn.reshape(1, -1))


def _mlp_kernel(hn_ref, wu_ref, wd_ref, h_ref, g_ref, y_ref, acc_ref):
    f = pl.program_id(1)

    @pl.when(f == 0)
    def _():
        acc_ref[...] = jnp.zeros_like(acc_ref)

    u = jnp.maximum(jnp.dot(hn_ref[...], wu_ref[...], preferred_element_type=F32), 0.0)
    acc_ref[...] += jnp.dot((u * u).astype(BF16), wd_ref[...], preferred_element_type=F32)

    @pl.when(f == pl.num_programs(1) - 1)
    def _():
        y_ref[...] = _rms(h_ref[...] + acc_ref[...], g_ref[...])


def _mlp(hn, w_up_bf, w_down_bf, h, g_final, tm=512, tf=512):
    m, d = h.shape
    d_ff = w_up_bf.shape[1]
    blk = (_nbytes((tm, d), BF16) + _nbytes((d, tf), BF16) + _nbytes((tf, d), BF16)
           + 2 * _nbytes((tm, d), F32))
    return pl.pallas_call(
        _mlp_kernel,
        grid=(m // tm, d_ff // tf),
        in_specs=[pl.BlockSpec((tm, d), lambda i, f: (i, 0)),
                  pl.BlockSpec((d, tf), lambda i, f: (0, f)),
                  pl.BlockSpec((tf, d), lambda i, f: (f, 0)),
                  pl.BlockSpec((tm, d), lambda i, f: (i, 0)),
                  pl.BlockSpec((1, d), lambda i, f: (0, 0))],
        out_specs=pl.BlockSpec((tm, d), lambda i, f: (i, 0)),
        out_shape=jax.ShapeDtypeStruct((m, d), F32),
        scratch_shapes=[pltpu.VMEM((tm, d), F32)],
        compiler_params=_params(blk, _nbytes((tm, d), F32),
                                semantics=("arbitrary", "arbitrary")),
        name="mlp",
    )(hn, w_up_bf, w_down_bf, h, g_final.reshape(1, d))


def _cumsum_seq_kernel(x_ref, o_ref):
    s = x_ref.shape[1]
    r = lax.broadcasted_iota(jnp.int32, (LANES, 3 * LANES), 0)
    c = lax.broadcasted_iota(jnp.int32, (LANES, 3 * LANES), 1) & (LANES - 1)
    tri3 = jnp.where(c <= r, 1.0, 0.0).astype(BF16)
    carry = jnp.zeros((1, LANES), F32)
    for ch in range(s // LANES):
        x = x_ref[0, ch * LANES:(ch + 1) * LANES, :]
        hi, mid, lo = _split3(x)
        loc = jnp.dot(tri3, jnp.concatenate([hi, mid, lo], axis=0),
                      preferred_element_type=F32)
        o_ref[0, ch * LANES:(ch + 1) * LANES, :] = loc + carry
        carry = carry + loc[LANES - 1:LANES, :]


def _cumsum_seq(x):
    b, s, _ = x.shape
    blk = 2 * _nbytes((s, LANES), F32)
    return pl.pallas_call(
        _cumsum_seq_kernel,
        grid=(b,),
        in_specs=[pl.BlockSpec((1, s, LANES), lambda i: (i, 0, 0))],
        out_specs=pl.BlockSpec((1, s, LANES), lambda i: (i, 0, 0)),
        out_shape=jax.ShapeDtypeStruct(x.shape, F32),
        compiler_params=_params(blk, semantics=("arbitrary",)),
        name="cumsum_seq",
    )(x)


def _to_bf16_once(k_ref, v_ref, kbf, vbf):
    @pl.when(pl.program_id(2) == 0)
    def _():
        kbf[...] = k_ref[0].astype(BF16)
        vbf[...] = v_ref[0].astype(BF16)


def _fox_prompt_kernel(q_ref, k_ref, v_ref, cq_ref, ck_ref, o_ref, kbf, vbf, *, tq):
    h = pl.program_id(1)
    i = pl.program_id(2)
    _to_bf16_once(k_ref, v_ref, kbf, vbf)
    q = q_ref[0].astype(BF16)
    lane = lax.broadcasted_iota(jnp.int32, (tq, LANES), 1)
    cq = jnp.sum(jnp.where(lane == h, cq_ref[0], 0.0), axis=1, keepdims=True)

    def block(kb, carry, masked):
        m, l, acc = carry
        off = pl.multiple_of(kb * tq, tq)
        k = kbf[pl.ds(off, tq), :]
        v = vbf[pl.ds(off, tq), :]
        s = lax.dot_general(q, k, NT_DIMS, preferred_element_type=F32) * SCALE
        ck = ck_ref[0, pl.ds(h, 1), pl.ds(off, tq)]
        s = s + cq - ck
        if masked:
            qpos = lax.broadcasted_iota(jnp.int32, (tq, tq), 0)
            kpos = lax.broadcasted_iota(jnp.int32, (tq, tq), 1)
            s = jnp.where(kpos <= qpos, s, -jnp.inf)
        m_new = jnp.maximum(m, jnp.max(s, axis=1, keepdims=True))
        alpha = jnp.exp(m - m_new)
        p = jnp.exp(s - m_new)
        l = alpha * l + jnp.sum(p, axis=1, keepdims=True)
        acc = alpha * acc + jnp.dot(p.astype(BF16), v, preferred_element_type=F32)
        return m_new, l, acc

    carry = (jnp.full((tq, 1), -jnp.inf, F32), jnp.zeros((tq, 1), F32),
             jnp.zeros((tq, HEAD_DIM), F32))
    carry = block(i, carry, True)
    _, l, acc = lax.fori_loop(0, i, lambda kb, c: block(kb, c, False), carry)
    o_ref[0] = acc / l


def _sb_prompt_kernel(q_ref, k_ref, v_ref, o_ref, kbf, vbf, *, tq):
    i = pl.program_id(2)
    _to_bf16_once(k_ref, v_ref, kbf, vbf)
    q = q_ref[0].astype(BF16)
    tri2 = _tri_rhs(2, suffix=True)
    nsub = tq // LANES

    def block(kb, carry, masked):
        later, acc = carry
        off = pl.multiple_of(kb * LANES, LANES)
        k = kbf[pl.ds(off, LANES), :]
        v = vbf[pl.ds(off, LANES), :]
        z = lax.dot_general(q, k, NT_DIMS, preferred_element_type=F32) * SCALE
        log_beta, log_1m = _log_sigmoid_pair(z)
        if masked:
            qpos = i * tq + lax.broadcasted_iota(jnp.int32, (tq, LANES), 0)
            kpos = kb * LANES + lax.broadcasted_iota(jnp.int32, (tq, LANES), 1)
            strict = kpos < qpos
            log_1m = jnp.where(strict, log_1m, 0.0)
        a = jnp.exp(log_beta + _suffix_lanes(log_1m, tri2) + later)
        if masked:
            a = jnp.where(strict, a, 0.0)
        acc = acc + jnp.dot(a.astype(BF16), v, preferred_element_type=F32)
        later = later + jnp.sum(log_1m, axis=1, keepdims=True)
        return later, acc

    carry = (jnp.zeros((tq, 1), F32), jnp.zeros((tq, HEAD_DIM), F32))
    for d in range(nsub):
        carry = block((i + 1) * nsub - 1 - d, carry, True)
    n_before = i * nsub
    _, acc = lax.fori_loop(0, n_before,
                           lambda j, c: block(n_before - 1 - j, c, False), carry)
    o_ref[0] = acc


def _prompt_attention(kernel, q, k, v, extra, extra_specs, tq, name):
    b, s, _ = q.shape
    head_rows = lambda bi, h, i: (bi, i, h)
    head_all = lambda bi, h, i: (bi, 0, h)
    blk = (2 * _nbytes((tq, HEAD_DIM), F32) + 2 * _nbytes((s, HEAD_DIM), F32)
           + _nbytes((tq, LANES), F32) + _nbytes((N_HEAD, s), F32))
    scratch = 2 * _nbytes((s, HEAD_DIM), BF16)
    return pl.pallas_call(
        functools.partial(kernel, tq=tq),
        grid=(b, N_HEAD, s // tq),
        in_specs=[pl.BlockSpec((1, tq, HEAD_DIM), head_rows),
                  pl.BlockSpec((1, s, HEAD_DIM), head_all),
                  pl.BlockSpec((1, s, HEAD_DIM), head_all)] + extra_specs,
        out_specs=pl.BlockSpec((1, tq, HEAD_DIM), head_rows),
        out_shape=jax.ShapeDtypeStruct(q.shape, F32),
        scratch_shapes=[pltpu.VMEM((s, HEAD_DIM), BF16), pltpu.VMEM((s, HEAD_DIM), BF16)],
        compiler_params=_params(blk, scratch,
                                semantics=("arbitrary", "arbitrary", "arbitrary")),
        name=name,
    )(q, k, v, *extra)


def _fox_prompt(q, k, v, c, c_t, tq=256):
    s = q.shape[1]
    specs = [pl.BlockSpec((1, tq, LANES), lambda bi, h, i: (bi, i, 0)),
             pl.BlockSpec((1, N_HEAD, s), lambda bi, h, i: (bi, 0, 0))]
    return _prompt_attention(_fox_prompt_kernel, q, k, v, (c, c_t), specs, tq, "fox_prompt")


def _sb_prompt(q, k, v, tq=256):
    return _prompt_attention(_sb_prompt_kernel, q, k, v, (), [], tq, "sb_prompt")


def _head_rows(page_ref, h, n):
    return page_ref[0, pl.ds(h, n, stride=N_HEAD), :]


def _stack_heads(rows8):
    t = 8
    return jnp.concatenate(
        [jnp.broadcast_to(rows8[h:h + 1, :], (t, LANES)) for h in range(N_HEAD)], axis=0)


def _decode_scores(q_ref, kn_ref, k_refs, page):
    t = q_ref.shape[1]
    q = q_ref[0].astype(BF16)
    qh = [q[:, h * HEAD_DIM:(h + 1) * HEAD_DIM] for h in range(N_HEAD)]
    chunks = []
    for kr in k_refs:
        per_head = [lax.dot_general(qh[h], _head_rows(kr, h, page).astype(BF16), NT_DIMS,
                                    preferred_element_type=F32) for h in range(N_HEAD)]
        chunks.append(jnp.concatenate(per_head, axis=0) * SCALE)
    kn = jnp.concatenate([kn_ref[0], jnp.zeros((page - t, W_MIX), F32)], axis=0).astype(BF16)
    per_head = [lax.dot_general(qh[h], kn[:, h * HEAD_DIM:(h + 1) * HEAD_DIM], NT_DIMS,
                                preferred_element_type=F32) for h in range(N_HEAD)]
    chunks.append(jnp.concatenate(per_head, axis=0) * SCALE)
    return chunks


def _decode_apply(w_chunks, vn_ref, v_refs, page):
    t = vn_ref.shape[1]
    acc = [jnp.zeros((t, HEAD_DIM), F32) for _ in range(N_HEAD)]
    for w, vr in zip(w_chunks[:-1], v_refs):
        for h in range(N_HEAD):
            acc[h] = acc[h] + jnp.dot(w[h * t:(h + 1) * t, :].astype(BF16),
                                      _head_rows(vr, h, page).astype(BF16),
                                      preferred_element_type=F32)
    vn = jnp.concatenate([vn_ref[0], jnp.zeros((page - t, W_MIX), F32)], axis=0).astype(BF16)
    w = w_chunks[-1]
    for h in range(N_HEAD):
        acc[h] = acc[h] + jnp.dot(w[h * t:(h + 1) * t, :].astype(BF16),
                                  vn[:, h * HEAD_DIM:(h + 1) * HEAD_DIM],
                                  preferred_element_type=F32)
    return acc


def _fox_decode_kernel(pt_ref, q_ref, kn_ref, vn_ref, lfn_ref, *refs, n_pages, page):
    del pt_ref
    k_refs = refs[:n_pages]
    v_refs = refs[n_pages:2 * n_pages]
    lf_refs = refs[2 * n_pages:3 * n_pages]
    o_ref = refs[3 * n_pages]
    t = q_ref.shape[1]
    rows = N_HEAD * t

    tri3 = _tri_rhs(3, suffix=False)
    carry = jnp.zeros((N_HEAD, 1), F32)
    c_chunks = []
    for lr in lf_refs:
        loc = _prefix_lanes(lr[0], tri3)
        c_chunks.append(loc + carry)
        carry = carry + loc[:, LANES - 1:LANES]
    c_new = _prefix_lanes(lfn_ref[0], tri3) + carry
    c_chunks.append(c_new)

    sub = lax.broadcasted_iota(jnp.int32, (rows, LANES), 0) & (t - 1)
    lane = lax.broadcasted_iota(jnp.int32, (rows, LANES), 1)
    c_new_rows = _stack_heads(c_new)
    cq = jnp.sum(jnp.where(lane == sub, c_new_rows, 0.0), axis=1, keepdims=True)

    s_chunks = _decode_scores(q_ref, kn_ref, k_refs, page)
    s_chunks = [s + cq - _stack_heads(c) for s, c in zip(s_chunks, c_chunks)]
    s_chunks[-1] = jnp.where(lane <= sub, s_chunks[-1], -jnp.inf)

    mx = s_chunks[0]
    for s in s_chunks[1:]:
        mx = jnp.maximum(mx, s)
    m = jnp.max(mx, axis=1, keepdims=True)
    p_chunks = [jnp.exp(s - m) for s in s_chunks]
    tot = p_chunks[0]
    for p in p_chunks[1:]:
        tot = tot + p
    l = jnp.sum(tot, axis=1, keepdims=True)

    acc = _decode_apply(p_chunks, vn_ref, v_refs, page)
    o_ref[0] = jnp.concatenate(
        [acc[h] / l[h * t:(h + 1) * t, :] for h in range(N_HEAD)], axis=1)


def _sb_decode_kernel(pt_ref, q_ref, kn_ref, vn_ref, *refs, n_pages, page):
    del pt_ref
    k_refs = refs[:n_pages]
    v_refs = refs[n_pages:2 * n_pages]
    o_ref = refs[2 * n_pages]
    t = q_ref.shape[1]
    rows = N_HEAD * t

    sub = lax.broadcasted_iota(jnp.int32, (rows, LANES), 0) & (t - 1)
    lane = lax.broadcasted_iota(jnp.int32, (rows, LANES), 1)
    strict_new = lane < sub

    z_chunks = _decode_scores(q_ref, kn_ref, k_refs, page)
    log_beta, log_1m = [], []
    for z in z_chunks:
        lb, l1 = _log_sigmoid_pair(z)
        log_beta.append(lb)
        log_1m.append(l1)
    log_1m[-1] = jnp.where(strict_new, log_1m[-1], 0.0)

    tri2 = _tri_rhs(2, suffix=True)
    later = jnp.zeros((rows, 1), F32)
    a_chunks = [None] * len(z_chunks)
    for c in range(len(z_chunks) - 1, -1, -1):
        a = jnp.exp(log_beta[c] + _suffix_lanes(log_1m[c], tri2) + later)
        if c == len(z_chunks) - 1:
            a = jnp.where(strict_new, a, 0.0)
        a_chunks[c] = a
        later = later + jnp.sum(log_1m[c], axis=1, keepdims=True)

    acc = _decode_apply(a_chunks, vn_ref, v_refs, page)
    o_ref[0] = jnp.concatenate(acc, axis=1)


def _decode_attention(kernel, name, page_table, q, kn, vn, cache_k, cache_v, lf_new=None,
                      cache_lf=None):
    b, t, _ = q.shape
    n_pages = page_table.shape[1]
    page = cache_k.shape[1] // N_HEAD
    tok = pl.BlockSpec((1, t, W_MIX), lambda bi, pt: (bi, 0, 0))

    def paged(shape):
        return [pl.BlockSpec((1,) + shape, lambda bi, pt, p=p: (pt[bi, p], 0, 0))
                for p in range(n_pages)]

    in_specs = [tok, tok, tok]
    args = [q, kn, vn]
    blk = 4 * _nbytes((t, W_MIX), F32) + 2 * n_pages * _nbytes((page * N_HEAD, HEAD_DIM), F32)
    if cache_lf is not None:
        in_specs.append(pl.BlockSpec((1, N_HEAD, LANES), lambda bi, pt: (bi, 0, 0)))
        args.append(lf_new)
    in_specs += paged((page * N_HEAD, HEAD_DIM)) + paged((page * N_HEAD, HEAD_DIM))
    args += [cache_k] * n_pages + [cache_v] * n_pages
    if cache_lf is not None:
        in_specs += paged((N_HEAD, page))
        args += [cache_lf] * n_pages
        blk += (n_pages + 1) * _nbytes((N_HEAD, LANES), F32)
    return pl.pallas_call(
        functools.partial(kernel, n_pages=n_pages, page=page),
        grid_spec=pltpu.PrefetchScalarGridSpec(
            num_scalar_prefetch=1,
            grid=(b,),
            in_specs=in_specs,
            out_specs=pl.BlockSpec((1, t, W_MIX), lambda bi, pt: (bi, 0, 0)),
        ),
        out_shape=jax.ShapeDtypeStruct(q.shape, F32),
        compiler_params=_params(blk, semantics=("arbitrary",)),
        name=name,
    )(page_table, *args)


def _project(x2d, g_mix, w_in_bf, wg_bf, bg):
    xn = _norm_rows(x2d, g_mix)
    segs = [_proj_segment(xn, w_in_bf, s) for s in range(6)]
    return segs, _gate_logf(xn, wg_bf, bg)


def _finish(o_f, o_s, x2d, g_out_fox, g_out_sb, w_out_bf, g_ffn, w_up_bf, w_down_bf, g_final):
    h, hn = _attn_out(o_f, o_s, g_out_fox, g_out_sb, x2d, w_out_bf, g_ffn)
    return _mlp(hn, w_up_bf, w_down_bf, h, g_final)


def kernel(x_prompt, x_sample, cache_fox_k, cache_fox_v, cache_fox_logf, cache_sb_k, cache_sb_v,
           page_table, g_mix, w_in, b_f, g_out_fox, g_out_sb, w_out, g_ffn, w_up, w_down, g_final):
    depth = w_in.shape[0]
    assert depth == 1, "single-layer trunk"
    b, s, d = x_prompt.shape
    db, t, _ = x_sample.shape
    n_pool, page = cache_fox_k.shape[1], cache_fox_k.shape[2]
    assert page == LANES and cache_fox_k.shape[3:] == (N_HEAD, HEAD_DIM)

    n_main = 6 * W_MIX
    w_in_bf = w_in[0, :, :n_main].astype(BF16)
    wg_bf = jnp.pad(w_in[0, :, n_main:], ((0, 0), (0, LANES - N_HEAD))).astype(BF16)
    bg = jnp.pad(b_f[0], (0, LANES - N_HEAD)).reshape(1, LANES).astype(F32)
    w_out_bf = w_out[0].astype(BF16)
    w_up_bf = w_up[0].astype(BF16)
    w_down_bf = w_down[0].astype(BF16)
    tail = (g_out_fox[0], g_out_sb[0], w_out_bf, g_ffn[0], w_up_bf, w_down_bf, g_final)

    xp = x_prompt.reshape(b * s, d)
    (qf, kf, vf, qs, ks, vs), lf = _project(xp, g_mix[0], w_in_bf, wg_bf, bg)
    as_seq = lambda a: a.reshape(b, s, a.shape[-1])
    c = _cumsum_seq(as_seq(lf))
    c_t = jnp.swapaxes(c[:, :, :N_HEAD], 1, 2)
    o_f = _fox_prompt(as_seq(qf), as_seq(kf), as_seq(vf), c, c_t)
    o_s = _sb_prompt(as_seq(qs), as_seq(ks), as_seq(vs))
    y_prompt = _finish(o_f.reshape(b * s, W_MIX), o_s.reshape(b * s, W_MIX), xp, *tail)

    xs = x_sample.reshape(db * t, d)
    (qf_s, kf_s, vf_s, qs_s, ks_s, vs_s), lf_s = _project(xs, g_mix[0], w_in_bf, wg_bf, bg)
    as_tok = lambda a: a.reshape(db, t, a.shape[-1])
    rows_view = lambda cache: cache[0].reshape(n_pool, page * N_HEAD, HEAD_DIM)
    lf_cache_t = jnp.swapaxes(cache_fox_logf[0], 1, 2)
    lf_new_t = jnp.swapaxes(as_tok(lf_s)[:, :, :N_HEAD], 1, 2)
    lf_new_t = jnp.pad(lf_new_t, ((0, 0), (0, 0), (0, LANES - t)))
    o_f_s = _decode_attention(_fox_decode_kernel, "fox_decode", page_table,
                              as_tok(qf_s), as_tok(kf_s), as_tok(vf_s),
                              rows_view(cache_fox_k), rows_view(cache_fox_v),
                              lf_new=lf_new_t, cache_lf=lf_cache_t)
    o_s_s = _decode_attention(_sb_decode_kernel, "sb_decode", page_table,
                              as_tok(qs_s), as_tok(ks_s), as_tok(vs_s),
                              rows_view(cache_sb_k), rows_view(cache_sb_v))
    y_sample = _finish(o_f_s.reshape(db * t, W_MIX), o_s_s.reshape(db * t, W_MIX), xs, *tail)

    heads = lambda a, n0, n1: a.reshape(1, n0, n1, N_HEAD, HEAD_DIM)
    return (y_prompt.reshape(b, s, d), y_sample.reshape(db, t, d),
            heads(kf, b, s), heads(vf, b, s), lf[:, :N_HEAD].reshape(1, b, s, N_HEAD),
            heads(ks, b, s), heads(vs, b, s),
            heads(kf_s, db, t), heads(vf_s, db, t), lf_s[:, :N_HEAD].reshape(1, db, t, N_HEAD),
            heads(ks_s, db, t), heads(vs_s, db, t))
```

```python
import functools

import jax
import jax.numpy as jnp
from jax import lax
from jax.experimental import pallas as pl
from jax.experimental.pallas import tpu as pltpu

F32 = jnp.float32
BF16 = jnp.bfloat16

EPS = 1e-6
HEAD_DIM = 128
N_HEAD = 8
W_MIX = N_HEAD * HEAD_DIM
SCALE = HEAD_DIM ** -0.5
LANES = 128
V7X_VMEM_BYTES = 64 * 1024 * 1024
COMPILER_TEMP_BYTES = 12 * 1024 * 1024

NT_DIMS = (((1,), (1,)), ((), ()))


def _vmem_limit(block_bytes, scratch_bytes=0):
    need = 2 * block_bytes + scratch_bytes + COMPILER_TEMP_BYTES
    return int(min(need, V7X_VMEM_BYTES - 4 * 1024 * 1024))


def _nbytes(shape, dtype):
    n = 1
    for s in shape:
        n *= s
    return n * jnp.dtype(dtype).itemsize


def _params(block_bytes, scratch_bytes=0, semantics=None):
    return pltpu.CompilerParams(
        dimension_semantics=semantics,
        vmem_limit_bytes=_vmem_limit(block_bytes, scratch_bytes))


def _rms(x, g):
    ms = jnp.mean(x * x, axis=-1, keepdims=True)
    return x * lax.rsqrt(ms + EPS) * g


def _log_sigmoid_pair(z):
    log_sig = jnp.minimum(z, 0.0) - jnp.log(1.0 + jnp.exp(-jnp.abs(z)))
    return log_sig, log_sig - z


def _tri_rhs(copies, suffix):
    r = lax.broadcasted_iota(jnp.int32, (copies * LANES, LANES), 0) & (LANES - 1)
    c = lax.broadcasted_iota(jnp.int32, (copies * LANES, LANES), 1)
    cond = (r > c) if suffix else (r <= c)
    return jnp.where(cond, 1.0, 0.0).astype(BF16)


def _split3(x):
    hi = x.astype(BF16)
    r = x - hi.astype(F32)
    mid = r.astype(BF16)
    lo = (r - mid.astype(F32)).astype(BF16)
    return hi, mid, lo


def _prefix_lanes(x, tri3):
    hi, mid, lo = _split3(x)
    return jnp.dot(jnp.concatenate([hi, mid, lo], axis=1), tri3,
                   preferred_element_type=F32)


def _suffix_lanes(x, tri2):
    hi = x.astype(BF16)
    lo = (x - hi.astype(F32)).astype(BF16)
    return jnp.dot(jnp.concatenate([hi, lo], axis=1), tri2,
                   preferred_element_type=F32)


def _norm_kernel(x_ref, g_ref, o_ref):
    o_ref[...] = _rms(x_ref[...], g_ref[...]).astype(o_ref.dtype)


def _norm_rows(x, g, tm=512):
    m, d = x.shape
    blk = _nbytes((tm, d), F32) + _nbytes((tm, d), BF16)
    return pl.pallas_call(
        _norm_kernel,
        grid=(m // tm,),
        in_specs=[pl.BlockSpec((tm, d), lambda i: (i, 0)),
                  pl.BlockSpec((1, d), lambda i: (0, 0))],
        out_specs=pl.BlockSpec((tm, d), lambda i: (i, 0)),
        out_shape=jax.ShapeDtypeStruct((m, d), BF16),
        compiler_params=_params(blk, semantics=("arbitrary",)),
        name="norm_rows",
    )(x, g.reshape(1, d))


def _proj_kernel(x_ref, w_ref, o_ref):
    o_ref[...] = jnp.dot(x_ref[...], w_ref[...], preferred_element_type=F32)


def _proj_segment(xn, w_bf, seg, tm=512):
    m, d = xn.shape
    blk = _nbytes((tm, d), BF16) + _nbytes((d, W_MIX), BF16) + _nbytes((tm, W_MIX), F32)
    return pl.pallas_call(
        _proj_kernel,
        grid=(m // tm,),
        in_specs=[pl.BlockSpec((tm, d), lambda i: (i, 0)),
                  pl.BlockSpec((d, W_MIX), lambda i: (0, seg))],
        out_specs=pl.BlockSpec((tm, W_MIX), lambda i: (i, 0)),
        out_shape=jax.ShapeDtypeStruct((m, W_MIX), F32),
        compiler_params=_params(blk, semantics=("arbitrary",)),
        name=f"proj_seg{seg}",
    )(xn, w_bf)


def _gate_kernel(x_ref, w_ref, b_ref, o_ref):
    y = jnp.dot(x_ref[...], w_ref[...], preferred_element_type=F32) + b_ref[...]
    o_ref[...] = _log_sigmoid_pair(y)[0]


def _gate_logf(xn, wg_bf, bg, tm=512):
    m, d = xn.shape
    blk = _nbytes((tm, d), BF16) + _nbytes((d, LANES), BF16) + _nbytes((tm, LANES), F32)
    return pl.pallas_call(
        _gate_kernel,
        grid=(m // tm,),
        in_specs=[pl.BlockSpec((tm, d), lambda i: (i, 0)),
                  pl.BlockSpec((d, LANES), lambda i: (0, 0)),
                  pl.BlockSpec((1, LANES), lambda i: (0, 0))],
        out_specs=pl.BlockSpec((tm, LANES), lambda i: (i, 0)),
        out_shape=jax.ShapeDtypeStruct((m, LANES), F32),
        compiler_params=_params(blk, semantics=("arbitrary",)),
        name="gate_logf",
    )(xn, wg_bf, bg)


def _attn_out_kernel(of_ref, os_ref, gf_ref, gs_ref, x_ref, w_ref, gn_ref, h_ref, hn_ref):
    nf = _rms(of_ref[...], gf_ref[...]).astype(BF16)
    ns = _rms(os_ref[...], gs_ref[...]).astype(BF16)
    proj = (jnp.dot(nf, w_ref[0:W_MIX, :], preferred_element_type=F32)
            + jnp.dot(ns, w_ref[W_MIX:2 * W_MIX, :], preferred_element_type=F32))
    h = x_ref[...] + proj
    h_ref[...] = h
    hn_ref[...] = _rms(h, gn_ref[...]).astype(BF16)


def _attn_out(o_f, o_s, g_f, g_s, x, w_out_bf, g_ffn, tm=256):
    m, d = x.shape
    blk = (2 * _nbytes((tm, W_MIX), F32) + 2 * _nbytes((tm, d), F32)
           + _nbytes((2 * W_MIX, d), BF16) + _nbytes((tm, d), BF16))
    row = lambda i: (i, 0)
    const = lambda i: (0, 0)
    return pl.pallas_call(
        _attn_out_kernel,
        grid=(m // tm,),
        in_specs=[pl.BlockSpec((tm, W_MIX), row), pl.BlockSpec((tm, W_MIX), row),
                  pl.BlockSpec((1, W_MIX), const), pl.BlockSpec((1, W_MIX), const),
                  pl.BlockSpec((tm, d), row),
                  pl.BlockSpec((2 * W_MIX, d), const),
                  pl.BlockSpec((1, d), const)],
        out_specs=[pl.BlockSpec((tm, d), row), pl.BlockSpec((tm, d), row)],
        out_shape=[jax.ShapeDtypeStruct((m, d), F32), jax.ShapeDtypeStruct((m, d), BF16)],
        compiler_params=_params(blk, semantics=("arbitrary",)),
        name="attn_out",
    )(o_f, o_s, g_f.reshape(1, -1), g_s.reshape(1, -1), x, w_out_bf, g_ffn.reshape(1, -1))


def _mlp_kernel(hn_ref, wu_ref, wd_ref, h_ref, g_ref, y_ref, acc_ref):
    f = pl.program_id(1)

    @pl.when(f == 0)
    def _():
        acc_ref[...] = jnp.zeros_like(acc_ref)

    u = jnp.maximum(jnp.dot(hn_ref[...], wu_ref[...], preferred_element_type=F32), 0.0)
    acc_ref[...] += jnp.dot((u * u).astype(BF16), wd_ref[...], preferred_element_type=F32)

    @pl.when(f == pl.num_programs(1) - 1)
    def _():
        y_ref[...] = _rms(h_ref[...] + acc_ref[...], g_ref[...])


def _mlp(hn, w_up_bf, w_down_bf, h, g_final, tm=512, tf=512):
    m, d = h.shape
    d_ff = w_up_bf.shape[1]
    blk = (_nbytes((tm, d), BF16) + _nbytes((d, tf), BF16) + _nbytes((tf, d), BF16)
           + 2 * _nbytes((tm, d), F32))
    return pl.pallas_call(
        _mlp_kernel,
        grid=(m // tm, d_ff // tf),
        in_specs=[pl.BlockSpec((tm, d), lambda i, f: (i, 0)),
                  pl.BlockSpec((d, tf), lambda i, f: (0, f)),
                  pl.BlockSpec((tf, d), lambda i, f: (f, 0)),
                  pl.BlockSpec((tm, d), lambda i, f: (i, 0)),
                  pl.BlockSpec((1, d), lambda i, f: (0, 0))],
        out_specs=pl.BlockSpec((tm, d), lambda i, f: (i, 0)),
        out_shape=jax.ShapeDtypeStruct((m, d), F32),
        scratch_shapes=[pltpu.VMEM((tm, d), F32)],
        compiler_params=_params(blk, _nbytes((tm, d), F32),
                                semantics=("arbitrary", "arbitrary")),
        name="mlp",
    )(hn, w_up_bf, w_down_bf, h, g_final.reshape(1, d))


def _cumsum_seq_kernel(x_ref, o_ref):
    s = x_ref.shape[1]
    r = lax.broadcasted_iota(jnp.int32, (LANES, 3 * LANES), 0)
    c = lax.broadcasted_iota(jnp.int32, (LANES, 3 * LANES), 1) & (LANES - 1)
    tri3 = jnp.where(c <= r, 1.0, 0.0).astype(BF16)
    carry = jnp.zeros((1, LANES), F32)
    for ch in range(s // LANES):
        x = x_ref[0, ch * LANES:(ch + 1) * LANES, :]
        hi, mid, lo = _split3(x)
        loc = jnp.dot(tri3, jnp.concatenate([hi, mid, lo], axis=0),
                      preferred_element_type=F32)
        o_ref[0, ch * LANES:(ch + 1) * LANES, :] = loc + carry
        carry = carry + loc[LANES - 1:LANES, :]


def _cumsum_seq(x):
    b, s, _ = x.shape
    blk = 2 * _nbytes((s, LANES), F32)
    return pl.pallas_call(
        _cumsum_seq_kernel,
        grid=(b,),
        in_specs=[pl.BlockSpec((1, s, LANES), lambda i: (i, 0, 0))],
        out_specs=pl.BlockSpec((1, s, LANES), lambda i: (i, 0, 0)),
        out_shape=jax.ShapeDtypeStruct(x.shape, F32),
        compiler_params=_params(blk, semantics=("arbitrary",)),
        name="cumsum_seq",
    )(x)


def _fox_prompt_kernel(q_ref, k_ref, v_ref, c_ref, ct_ref, o_ref, qbf, kbf, vbf, *, tq):
    h = pl.program_id(1)
    s_len = q_ref.shape[1]
    qbf[...] = q_ref[0].astype(BF16)
    kbf[...] = k_ref[0].astype(BF16)
    vbf[...] = v_ref[0].astype(BF16)
    lane = lax.broadcasted_iota(jnp.int32, (tq, LANES), 1)
    qpos = lax.broadcasted_iota(jnp.int32, (tq, tq), 0)
    kpos = lax.broadcasted_iota(jnp.int32, (tq, tq), 1)
    causal = kpos <= qpos
    for i in range(s_len // tq):
        rows = slice(i * tq, (i + 1) * tq)
        q = qbf[rows, :]
        cq = jnp.sum(jnp.where(lane == h, c_ref[0, rows, :], 0.0), axis=1, keepdims=True)
        m = jnp.full((tq, 1), -jnp.inf, F32)
        l = jnp.zeros((tq, 1), F32)
        acc = jnp.zeros((tq, HEAD_DIM), F32)
        for kb in [i] + list(range(i)):
            cols = slice(kb * tq, (kb + 1) * tq)
            s = lax.dot_general(q, kbf[cols, :], NT_DIMS, preferred_element_type=F32) * SCALE
            s = s + cq - ct_ref[0, pl.ds(h, 1), cols]
            if kb == i:
                s = jnp.where(causal, s, -jnp.inf)
            m_new = jnp.maximum(m, jnp.max(s, axis=1, keepdims=True))
            alpha = jnp.exp(m - m_new)
            p = jnp.exp(s - m_new)
            l = alpha * l + jnp.sum(p, axis=1, keepdims=True)
            acc = alpha * acc + jnp.dot(p.astype(BF16), vbf[cols, :],
                                        preferred_element_type=F32)
            m = m_new
        o_ref[0, rows, :] = acc / l


def _sb_prompt_kernel(q_ref, k_ref, v_ref, o_ref, qbf, kbf, vbf, *, tq):
    s_len = q_ref.shape[1]
    qbf[...] = q_ref[0].astype(BF16)
    kbf[...] = k_ref[0].astype(BF16)
    vbf[...] = v_ref[0].astype(BF16)
    tri2 = _tri_rhs(2, suffix=True)
    qpos = lax.broadcasted_iota(jnp.int32, (tq, tq), 0)
    kpos = lax.broadcasted_iota(jnp.int32, (tq, tq), 1)
    strict = kpos < qpos
    for i in range(s_len // tq):
        rows = slice(i * tq, (i + 1) * tq)
        q = qbf[rows, :]
        later = jnp.zeros((tq, 1), F32)
        acc = jnp.zeros((tq, HEAD_DIM), F32)
        for kb in range(i, -1, -1):
            cols = slice(kb * tq, (kb + 1) * tq)
            z = lax.dot_general(q, kbf[cols, :], NT_DIMS, preferred_element_type=F32) * SCALE
            log_beta, log_1m = _log_sigmoid_pair(z)
            if kb == i:
                log_1m = jnp.where(strict, log_1m, 0.0)
            parts = [None] * (tq // LANES)
            for sub in range(tq // LANES - 1, -1, -1):
                ln = slice(sub * LANES, (sub + 1) * LANES)
                l1 = log_1m[:, ln]
                a = jnp.exp(log_beta[:, ln] + _suffix_lanes(l1, tri2) + later)
                if kb == i:
                    a = jnp.where(strict[:, ln], a, 0.0)
                parts[sub] = a.astype(BF16)
                later = later + jnp.sum(l1, axis=1, keepdims=True)
            acc = acc + jnp.dot(jnp.concatenate(parts, axis=1), vbf[cols, :],
                                preferred_element_type=F32)
        o_ref[0, rows, :] = acc


def _prompt_attention(kernel, q, k, v, extra, extra_specs, tq, name):
    b, s, _ = q.shape
    head = lambda bi, h: (bi, 0, h)
    blk = 4 * _nbytes((s, HEAD_DIM), F32) + _nbytes((s, LANES), F32) + _nbytes((N_HEAD, s), F32)
    scratch = 3 * _nbytes((s, HEAD_DIM), BF16)
    seq = pl.BlockSpec((1, s, HEAD_DIM), head)
    return pl.pallas_call(
        functools.partial(kernel, tq=tq),
        grid=(b, N_HEAD),
        in_specs=[seq, seq, seq] + extra_specs,
        out_specs=seq,
        out_shape=jax.ShapeDtypeStruct(q.shape, F32),
        scratch_shapes=[pltpu.VMEM((s, HEAD_DIM), BF16)] * 3,
        compiler_params=_params(blk, scratch, semantics=("arbitrary", "arbitrary")),
        name=name,
    )(q, k, v, *extra)


def _fox_prompt(q, k, v, c, c_t, tq=256):
    s = q.shape[1]
    specs = [pl.BlockSpec((1, s, LANES), lambda bi, h: (bi, 0, 0)),
             pl.BlockSpec((1, N_HEAD, s), lambda bi, h: (bi, 0, 0))]
    return _prompt_attention(_fox_prompt_kernel, q, k, v, (c, c_t), specs, tq, "fox_prompt")


def _sb_prompt(q, k, v, tq=256):
    return _prompt_attention(_sb_prompt_kernel, q, k, v, (), [], tq, "sb_prompt")


def _head_rows(page_ref, h, n):
    return page_ref[0, pl.ds(h, n, stride=N_HEAD), :]


def _stack_heads(rows8):
    t = 8
    return jnp.concatenate(
        [jnp.broadcast_to(rows8[h:h + 1, :], (t, LANES)) for h in range(N_HEAD)], axis=0)


def _decode_scores(q_ref, kn_ref, k_refs, page):
    t = q_ref.shape[1]
    q = q_ref[0].astype(BF16)
    qh = [q[:, h * HEAD_DIM:(h + 1) * HEAD_DIM] for h in range(N_HEAD)]
    chunks = []
    for kr in k_refs:
        per_head = [lax.dot_general(qh[h], _head_rows(kr, h, page).astype(BF16), NT_DIMS,
                                    preferred_element_type=F32) for h in range(N_HEAD)]
        chunks.append(jnp.concatenate(per_head, axis=0) * SCALE)
    kn = jnp.concatenate([kn_ref[0], jnp.zeros((page - t, W_MIX), F32)], axis=0).astype(BF16)
    per_head = [lax.dot_general(qh[h], kn[:, h * HEAD_DIM:(h + 1) * HEAD_DIM], NT_DIMS,
                                preferred_element_type=F32) for h in range(N_HEAD)]
    chunks.append(jnp.concatenate(per_head, axis=0) * SCALE)
    return chunks


def _decode_apply(w_chunks, vn_ref, v_refs, page):
    t = vn_ref.shape[1]
    acc = [jnp.zeros((t, HEAD_DIM), F32) for _ in range(N_HEAD)]
    for w, vr in zip(w_chunks[:-1], v_refs):
        for h in range(N_HEAD):
            acc[h] = acc[h] + jnp.dot(w[h * t:(h + 1) * t, :].astype(BF16),
                                      _head_rows(vr, h, page).astype(BF16),
                                      preferred_element_type=F32)
    vn = jnp.concatenate([vn_ref[0], jnp.zeros((page - t, W_MIX), F32)], axis=0).astype(BF16)
    w = w_chunks[-1]
    for h in range(N_HEAD):
        acc[h] = acc[h] + jnp.dot(w[h * t:(h + 1) * t, :].astype(BF16),
                                  vn[:, h * HEAD_DIM:(h + 1) * HEAD_DIM],
                                  preferred_element_type=F32)
    return acc


def _fox_decode_kernel(pt_ref, q_ref, kn_ref, vn_ref, lfn_ref, *refs, n_pages, page):
    del pt_ref
    k_refs = refs[:n_pages]
    v_refs = refs[n_pages:2 * n_pages]
    lf_refs = refs[2 * n_pages:3 * n_pages]
    o_ref = refs[3 * n_pages]
    t = q_ref.shape[1]
    rows = N_HEAD * t

    tri3 = _tri_rhs(3, suffix=False)
    carry = jnp.zeros((N_HEAD, 1), F32)
    c_chunks = []
    for lr in lf_refs:
        loc = _prefix_lanes(lr[0], tri3)
        c_chunks.append(loc + carry)
        carry = carry + loc[:, LANES - 1:LANES]
    c_new = _prefix_lanes(lfn_ref[0], tri3) + carry
    c_chunks.append(c_new)

    sub = lax.broadcasted_iota(jnp.int32, (rows, LANES), 0) & (t - 1)
    lane = lax.broadcasted_iota(jnp.int32, (rows, LANES), 1)
    c_new_rows = _stack_heads(c_new)
    cq = jnp.sum(jnp.where(lane == sub, c_new_rows, 0.0), axis=1, keepdims=True)

    s_chunks = _decode_scores(q_ref, kn_ref, k_refs, page)
    s_chunks = [s + cq - _stack_heads(c) for s, c in zip(s_chunks, c_chunks)]
    s_chunks[-1] = jnp.where(lane <= sub, s_chunks[-1], -jnp.inf)

    mx = s_chunks[0]
    for s in s_chunks[1:]:
        mx = jnp.maximum(mx, s)
    m = jnp.max(mx, axis=1, keepdims=True)
    p_chunks = [jnp.exp(s - m) for s in s_chunks]
    tot = p_chunks[0]
    for p in p_chunks[1:]:
        tot = tot + p
    l = jnp.sum(tot, axis=1, keepdims=True)

    acc = _decode_apply(p_chunks, vn_ref, v_refs, page)
    o_ref[0] = jnp.concatenate(
        [acc[h] / l[h * t:(h + 1) * t, :] for h in range(N_HEAD)], axis=1)


def _sb_decode_kernel(pt_ref, q_ref, kn_ref, vn_ref, *refs, n_pages, page):
    del pt_ref
    k_refs = refs[:n_pages]
    v_refs = refs[n_pages:2 * n_pages]
    o_ref = refs[2 * n_pages]
    t = q_ref.shape[1]
    rows = N_HEAD * t

    sub = lax.broadcasted_iota(jnp.int32, (rows, LANES), 0) & (t - 1)
    lane = lax.broadcasted_iota(jnp.int32, (rows, LANES), 1)
    strict_new = lane < sub

    z_chunks = _decode_scores(q_ref, kn_ref, k_refs, page)
    log_beta, log_1m = [], []
    for z in z_chunks:
        lb, l1 = _log_sigmoid_pair(z)
        log_beta.append(lb)
        log_1m.append(l1)
    log_1m[-1] = jnp.where(strict_new, log_1m[-1], 0.0)

    tri2 = _tri_rhs(2, suffix=True)
    later = jnp.zeros((rows, 1), F32)
    a_chunks = [None] * len(z_chunks)
    for c in range(len(z_chunks) - 1, -1, -1):
        a = jnp.exp(log_beta[c] + _suffix_lanes(log_1m[c], tri2) + later)
        if c == len(z_chunks) - 1:
            a = jnp.where(strict_new, a, 0.0)
        a_chunks[c] = a
        later = later + jnp.sum(log_1m[c], axis=1, keepdims=True)

    acc = _decode_apply(a_chunks, vn_ref, v_refs, page)
    o_ref[0] = jnp.concatenate(acc, axis=1)


def _decode_attention(kernel, name, page_table, q, kn, vn, cache_k, cache_v, lf_new=None,
                      cache_lf=None):
    b, t, _ = q.shape
    n_pages = page_table.shape[1]
    page = cache_k.shape[1] // N_HEAD
    tok = pl.BlockSpec((1, t, W_MIX), lambda bi, pt: (bi, 0, 0))

    def paged(shape):
        return [pl.BlockSpec((1,) + shape, lambda bi, pt, p=p: (pt[bi, p], 0, 0))
                for p in range(n_pages)]

    in_specs = [tok, tok, tok]
    args = [q, kn, vn]
    blk = 4 * _nbytes((t, W_MIX), F32) + 2 * n_pages * _nbytes((page * N_HEAD, HEAD_DIM), F32)
    if cache_lf is not None:
        in_specs.append(pl.BlockSpec((1, N_HEAD, LANES), lambda bi, pt: (bi, 0, 0)))
        args.append(lf_new)
    in_specs += paged((page * N_HEAD, HEAD_DIM)) + paged((page * N_HEAD, HEAD_DIM))
    args += [cache_k] * n_pages + [cache_v] * n_pages
    if cache_lf is not None:
        in_specs += paged((N_HEAD, page))
        args += [cache_lf] * n_pages
        blk += (n_pages + 1) * _nbytes((N_HEAD, LANES), F32)
    return pl.pallas_call(
        functools.partial(kernel, n_pages=n_pages, page=page),
        grid_spec=pltpu.PrefetchScalarGridSpec(
            num_scalar_prefetch=1,
            grid=(b,),
            in_specs=in_specs,
            out_specs=pl.BlockSpec((1, t, W_MIX), lambda bi, pt: (bi, 0, 0)),
        ),
        out_shape=jax.ShapeDtypeStruct(q.shape, F32),
        compiler_params=_params(blk, semantics=("arbitrary",)),
        name=name,
    )(page_table, *args)


def _project(x2d, g_mix, w_in_bf, wg_bf, bg):
    xn = _norm_rows(x2d, g_mix)
    segs = [_proj_segment(xn, w_in_bf, s) for s in range(6)]
    return segs, _gate_logf(xn, wg_bf, bg)


def _finish(o_f, o_s, x2d, g_out_fox, g_out_sb, w_out_bf, g_ffn, w_up_bf, w_down_bf, g_final):
    h, hn = _attn_out(o_f, o_s, g_out_fox, g_out_sb, x2d, w_out_bf, g_ffn)
    return _mlp(hn, w_up_bf, w_down_bf, h, g_final)


def kernel(x_prompt, x_sample, cache_fox_k, cache_fox_v, cache_fox_logf, cache_sb_k, cache_sb_v,
           page_table, g_mix, w_in, b_f, g_out_fox, g_out_sb, w_out, g_ffn, w_up, w_down, g_final):
    depth = w_in.shape[0]
    assert depth == 1, "single-layer trunk"
    b, s, d = x_prompt.shape
    db, t, _ = x_sample.shape
    n_pool, page = cache_fox_k.shape[1], cache_fox_k.shape[2]
    assert page == LANES and cache_fox_k.shape[3:] == (N_HEAD, HEAD_DIM)

    n_main = 6 * W_MIX
    w_in_bf = w_in[0, :, :n_main].astype(BF16)
    wg_bf = jnp.pad(w_in[0, :, n_main:], ((0, 0), (0, LANES - N_HEAD))).astype(BF16)
    bg = jnp.pad(b_f[0], (0, LANES - N_HEAD)).reshape(1, LANES).astype(F32)
    w_out_bf = w_out[0].astype(BF16)
    w_up_bf = w_up[0].astype(BF16)
    w_down_bf = w_down[0].astype(BF16)
    tail = (g_out_fox[0], g_out_sb[0], w_out_bf, g_ffn[0], w_up_bf, w_down_bf, g_final)

    xp = x_prompt.reshape(b * s, d)
    (qf, kf, vf, qs, ks, vs), lf = _project(xp, g_mix[0], w_in_bf, wg_bf, bg)
    as_seq = lambda a: a.reshape(b, s, a.shape[-1])
    c = _cumsum_seq(as_seq(lf))
    c_t = jnp.swapaxes(c[:, :, :N_HEAD], 1, 2)
    o_f = _fox_prompt(as_seq(qf), as_seq(kf), as_seq(vf), c, c_t)
    o_s = _sb_prompt(as_seq(qs), as_seq(ks), as_seq(vs))
    y_prompt = _finish(o_f.reshape(b * s, W_MIX), o_s.reshape(b * s, W_MIX), xp, *tail)

    xs = x_sample.reshape(db * t, d)
    (qf_s, kf_s, vf_s, qs_s, ks_s, vs_s), lf_s = _project(xs, g_mix[0], w_in_bf, wg_bf, bg)
    as_tok = lambda a: a.reshape(db, t, a.shape[-1])
    rows_view = lambda cache: cache[0].reshape(n_pool, page * N_HEAD, HEAD_DIM)
    lf_cache_t = jnp.swapaxes(cache_fox_logf[0], 1, 2)
    lf_new_t = jnp.swapaxes(as_tok(lf_s)[:, :, :N_HEAD], 1, 2)
    lf_new_t = jnp.pad(lf_new_t, ((0, 0), (0, 0), (0, LANES - t)))
    o_f_s = _decode_attention(_fox_decode_kernel, "fox_decode", page_table,
                              as_tok(qf_s), as_tok(kf_s), as_tok(vf_s),
                              rows_view(cache_fox_k), rows_view(cache_fox_v),
                              lf_new=lf_new_t, cache_lf=lf_cache_t)
    o_s_s = _decode_attention(_sb_decode_kernel, "sb_decode", page_table,
                              as_tok(qs_s), as_tok(ks_s), as_tok(vs_s),
                              rows_view(cache_sb_k), rows_view(cache_sb_v))
    y_sample = _finish(o_f_s.reshape(db * t, W_MIX), o_s_s.reshape(db * t, W_MIX), xs, *tail)

    heads = lambda a, n0, n1: a.reshape(1, n0, n1, N_HEAD, HEAD_DIM)
    return (y_prompt.reshape(b, s, d), y_sample.reshape(db, t, d),
            heads(kf, b, s), heads(vf, b, s), lf[:, :N_HEAD].reshape(1, b, s, N_HEAD),
            heads(ks, b, s), heads(vs, b, s),
            heads(kf_s, db, t), heads(vf_s, db, t), lf_s[:, :N_HEAD].reshape(1, db, t, N_HEAD),
            heads(ks_s, db, t), heads(vs_s, db, t))
```

```python
import functools

import jax
import jax.numpy as jnp
from jax import lax
from jax.experimental import pallas as pl
from jax.experimental.pallas import tpu as pltpu

F32 = jnp.float32
BF16 = jnp.bfloat16

EPS = 1e-6
HEAD_DIM = 128
N_HEAD = 8
W_MIX = N_HEAD * HEAD_DIM
SCALE = HEAD_DIM ** -0.5
LANES = 128
V7X_VMEM_BYTES = 64 * 1024 * 1024
COMPILER_TEMP_BYTES = 12 * 1024 * 1024

NT_DIMS = (((1,), (1,)), ((), ()))


def _vmem_limit(block_bytes, scratch_bytes=0):
    need = 2 * block_bytes + scratch_bytes + COMPILER_TEMP_BYTES
    return int(min(need, V7X_VMEM_BYTES - 4 * 1024 * 1024))


def _nbytes(shape, dtype):
    n = 1
    for s in shape:
        n *= s
    return n * jnp.dtype(dtype).itemsize


def _params(block_bytes, scratch_bytes=0, semantics=None):
    return pltpu.CompilerParams(
        dimension_semantics=semantics,
        vmem_limit_bytes=_vmem_limit(block_bytes, scratch_bytes))


def _rms(x, g):
    ms = jnp.mean(x * x, axis=-1, keepdims=True)
    return x * lax.rsqrt(ms + EPS) * g


def _log_sigmoid_pair(z):
    log_sig = jnp.minimum(z, 0.0) - jnp.log(1.0 + jnp.exp(-jnp.abs(z)))
    return log_sig, log_sig - z


def _tri_rhs(copies, suffix):
    r = lax.broadcasted_iota(jnp.int32, (copies * LANES, LANES), 0) & (LANES - 1)
    c = lax.broadcasted_iota(jnp.int32, (copies * LANES, LANES), 1)
    cond = (r > c) if suffix else (r <= c)
    return jnp.where(cond, 1.0, 0.0).astype(BF16)


def _split3(x):
    hi = x.astype(BF16)
    r = x - hi.astype(F32)
    mid = r.astype(BF16)
    lo = (r - mid.astype(F32)).astype(BF16)
    return hi, mid, lo


def _prefix_lanes(x, tri3):
    hi, mid, lo = _split3(x)
    return jnp.dot(jnp.concatenate([hi, mid, lo], axis=1), tri3,
                   preferred_element_type=F32)


def _suffix_lanes(x, tri2):
    hi = x.astype(BF16)
    lo = (x - hi.astype(F32)).astype(BF16)
    return jnp.dot(jnp.concatenate([hi, lo], axis=1), tri2,
                   preferred_element_type=F32)


def _norm_kernel(x_ref, g_ref, o_ref):
    o_ref[...] = _rms(x_ref[...], g_ref[...]).astype(o_ref.dtype)


def _norm_rows(x, g, tm=512):
    m, d = x.shape
    blk = _nbytes((tm, d), F32) + _nbytes((tm, d), BF16)
    return pl.pallas_call(
        _norm_kernel,
        grid=(m // tm,),
        in_specs=[pl.BlockSpec((tm, d), lambda i: (i, 0)),
                  pl.BlockSpec((1, d), lambda i: (0, 0))],
        out_specs=pl.BlockSpec((tm, d), lambda i: (i, 0)),
        out_shape=jax.ShapeDtypeStruct((m, d), BF16),
        compiler_params=_params(blk, semantics=("arbitrary",)),
        name="norm_rows",
    )(x, g.reshape(1, d))


N_SEG = 6


def _proj_kernel(x_ref, w_ref, wg_ref, bg_ref, *out_refs):
    j = pl.program_id(0)
    x = x_ref[...]
    for seg in range(N_SEG):
        @pl.when(j == seg)
        def _(seg=seg):
            out_refs[seg][...] = jnp.dot(x, w_ref[...], preferred_element_type=F32)

    @pl.when(j == 0)
    def _():
        y = jnp.dot(x, wg_ref[...], preferred_element_type=F32) + bg_ref[...]
        out_refs[N_SEG][...] = _log_sigmoid_pair(y)[0]


def _project(xn, w_bf, wg_bf, bg, tm=512):
    m, d = xn.shape
    n_i = m // tm

    def rows_of(seg):
        return lambda j, i: (jnp.where(j == seg, i, jnp.where(j < seg, 0, n_i - 1)), 0)

    blk = (_nbytes((tm, d), BF16) + _nbytes((d, W_MIX), BF16) + _nbytes((d, LANES), BF16)
           + N_SEG * _nbytes((tm, W_MIX), F32) + _nbytes((tm, LANES), F32))
    outs = pl.pallas_call(
        _proj_kernel,
        grid=(N_SEG, n_i),
        in_specs=[pl.BlockSpec((tm, d), lambda j, i: (i, 0)),
                  pl.BlockSpec((d, W_MIX), lambda j, i: (0, j)),
                  pl.BlockSpec((d, LANES), lambda j, i: (0, 0)),
                  pl.BlockSpec((1, LANES), lambda j, i: (0, 0))],
        out_specs=[pl.BlockSpec((tm, W_MIX), rows_of(seg)) for seg in range(N_SEG)]
                  + [pl.BlockSpec((tm, LANES), rows_of(0))],
        out_shape=[jax.ShapeDtypeStruct((m, W_MIX), F32)] * N_SEG
                  + [jax.ShapeDtypeStruct((m, LANES), F32)],
        compiler_params=_params(blk, semantics=("arbitrary", "arbitrary")),
        name="project",
    )(xn, w_bf, wg_bf, bg)
    return outs[:N_SEG], outs[N_SEG]


def _attn_out_kernel(of_ref, os_ref, gf_ref, gs_ref, x_ref, w_ref, gn_ref, h_ref, hn_ref):
    nf = _rms(of_ref[...], gf_ref[...]).astype(BF16)
    ns = _rms(os_ref[...], gs_ref[...]).astype(BF16)
    proj = (jnp.dot(nf, w_ref[0:W_MIX, :], preferred_element_type=F32)
            + jnp.dot(ns, w_ref[W_MIX:2 * W_MIX, :], preferred_element_type=F32))
    h = x_ref[...] + proj
    h_ref[...] = h
    hn_ref[...] = _rms(h, gn_ref[...]).astype(BF16)


def _attn_out(o_f, o_s, g_f, g_s, x, w_out_bf, g_ffn, tm=512):
    m, d = x.shape
    blk = (2 * _nbytes((tm, W_MIX), F32) + 2 * _nbytes((tm, d), F32)
           + _nbytes((2 * W_MIX, d), BF16) + _nbytes((tm, d), BF16))
    row = lambda i: (i, 0)
    const = lambda i: (0, 0)
    return pl.pallas_call(
        _attn_out_kernel,
        grid=(m // tm,),
        in_specs=[pl.BlockSpec((tm, W_MIX), row), pl.BlockSpec((tm, W_MIX), row),
                  pl.BlockSpec((1, W_MIX), const), pl.BlockSpec((1, W_MIX), const),
                  pl.BlockSpec((tm, d), row),
                  pl.BlockSpec((2 * W_MIX, d), const),
                  pl.BlockSpec((1, d), const)],
        out_specs=[pl.BlockSpec((tm, d), row), pl.BlockSpec((tm, d), row)],
        out_shape=[jax.ShapeDtypeStruct((m, d), F32), jax.ShapeDtypeStruct((m, d), BF16)],
        compiler_params=_params(blk, semantics=("arbitrary",)),
        name="attn_out",
    )(o_f, o_s, g_f.reshape(1, -1), g_s.reshape(1, -1), x, w_out_bf, g_ffn.reshape(1, -1))


def _mlp_kernel(hn_ref, wu_ref, wd_ref, h_ref, g_ref, y_ref):
    f = pl.program_id(1)

    @pl.when(f == 0)
    def _():
        y_ref[...] = h_ref[...]

    u = jnp.maximum(jnp.dot(hn_ref[...], wu_ref[...], preferred_element_type=F32), 0.0)
    y_ref[...] += jnp.dot((u * u).astype(BF16), wd_ref[...], preferred_element_type=F32)

    @pl.when(f == pl.num_programs(1) - 1)
    def _():
        y_ref[...] = _rms(y_ref[...], g_ref[...])


def _mlp(hn, w_up_bf, w_down_bf, h, g_final, tm=1024, tf=512):
    m, d = h.shape
    d_ff = w_up_bf.shape[1]
    blk = (_nbytes((tm, d), BF16) + _nbytes((d, tf), BF16) + _nbytes((tf, d), BF16)
           + 2 * _nbytes((tm, d), F32))
    return pl.pallas_call(
        _mlp_kernel,
        grid=(m // tm, d_ff // tf),
        in_specs=[pl.BlockSpec((tm, d), lambda i, f: (i, 0)),
                  pl.BlockSpec((d, tf), lambda i, f: (0, f)),
                  pl.BlockSpec((tf, d), lambda i, f: (f, 0)),
                  pl.BlockSpec((tm, d), lambda i, f: (i, 0)),
                  pl.BlockSpec((1, d), lambda i, f: (0, 0))],
        out_specs=pl.BlockSpec((tm, d), lambda i, f: (i, 0)),
        out_shape=jax.ShapeDtypeStruct((m, d), F32),
        compiler_params=_params(blk, semantics=("arbitrary", "arbitrary")),
        name="mlp",
    )(hn, w_up_bf, w_down_bf, h, g_final.reshape(1, d))


def _cumsum_seq_kernel(x_ref, o_ref):
    s = x_ref.shape[1]
    r = lax.broadcasted_iota(jnp.int32, (LANES, 3 * LANES), 0)
    c = lax.broadcasted_iota(jnp.int32, (LANES, 3 * LANES), 1) & (LANES - 1)
    tri3 = jnp.where(c <= r, 1.0, 0.0).astype(BF16)
    carry = jnp.zeros((1, LANES), F32)
    for ch in range(s // LANES):
        x = x_ref[0, ch * LANES:(ch + 1) * LANES, :]
        hi, mid, lo = _split3(x)
        loc = jnp.dot(tri3, jnp.concatenate([hi, mid, lo], axis=0),
                      preferred_element_type=F32)
        o_ref[0, ch * LANES:(ch + 1) * LANES, :] = loc + carry
        carry = carry + loc[LANES - 1:LANES, :]


def _cumsum_seq(x):
    b, s, _ = x.shape
    blk = 2 * _nbytes((s, LANES), F32)
    return pl.pallas_call(
        _cumsum_seq_kernel,
        grid=(b,),
        in_specs=[pl.BlockSpec((1, s, LANES), lambda i: (i, 0, 0))],
        out_specs=pl.BlockSpec((1, s, LANES), lambda i: (i, 0, 0)),
        out_shape=jax.ShapeDtypeStruct(x.shape, F32),
        compiler_params=_params(blk, semantics=("arbitrary",)),
        name="cumsum_seq",
    )(x)


def _fox_prompt_kernel(q_ref, k_ref, v_ref, c_ref, ct_ref, o_ref, qbf, kbf, vbf, *, tq):
    h = pl.program_id(1)
    s_len = q_ref.shape[1]
    qbf[...] = q_ref[0].astype(BF16)
    kbf[...] = k_ref[0].astype(BF16)
    vbf[...] = v_ref[0].astype(BF16)
    lane = lax.broadcasted_iota(jnp.int32, (tq, LANES), 1)
    qpos = lax.broadcasted_iota(jnp.int32, (tq, tq), 0)
    kpos = lax.broadcasted_iota(jnp.int32, (tq, tq), 1)
    causal = kpos <= qpos
    for i in range(s_len // tq):
        rows = slice(i * tq, (i + 1) * tq)
        q = qbf[rows, :]
        cq = jnp.sum(jnp.where(lane == h, c_ref[0, rows, :], 0.0), axis=1, keepdims=True)
        m = jnp.full((tq, 1), -jnp.inf, F32)
        l = jnp.zeros((tq, 1), F32)
        acc = jnp.zeros((tq, HEAD_DIM), F32)
        for kb in [i] + list(range(i)):
            cols = slice(kb * tq, (kb + 1) * tq)
            s = lax.dot_general(q, kbf[cols, :], NT_DIMS, preferred_element_type=F32) * SCALE
            s = s + cq - ct_ref[0, pl.ds(h, 1), cols]
            if kb == i:
                s = jnp.where(causal, s, -jnp.inf)
            m_new = jnp.maximum(m, jnp.max(s, axis=1, keepdims=True))
            alpha = jnp.exp(m - m_new)
            p = jnp.exp(s - m_new)
            l = alpha * l + jnp.sum(p, axis=1, keepdims=True)
            acc = alpha * acc + jnp.dot(p.astype(BF16), vbf[cols, :],
                                        preferred_element_type=F32)
            m = m_new
        o_ref[0, rows, :] = acc / l


def _sb_prompt_kernel(q_ref, k_ref, v_ref, o_ref, qbf, kbf, vbf, *, tq):
    s_len = q_ref.shape[1]
    qbf[...] = q_ref[0].astype(BF16)
    kbf[...] = k_ref[0].astype(BF16)
    vbf[...] = v_ref[0].astype(BF16)
    tri2 = _tri_rhs(2, suffix=True)
    qpos = lax.broadcasted_iota(jnp.int32, (tq, tq), 0)
    kpos = lax.broadcasted_iota(jnp.int32, (tq, tq), 1)
    strict = kpos < qpos
    for i in range(s_len // tq):
        rows = slice(i * tq, (i + 1) * tq)
        q = qbf[rows, :]
        later = jnp.zeros((tq, 1), F32)
        acc = jnp.zeros((tq, HEAD_DIM), F32)
        for kb in range(i, -1, -1):
            cols = slice(kb * tq, (kb + 1) * tq)
            z = lax.dot_general(q, kbf[cols, :], NT_DIMS, preferred_element_type=F32) * SCALE
            log_beta, log_1m = _log_sigmoid_pair(z)
            if kb == i:
                log_1m = jnp.where(strict, log_1m, 0.0)
            parts = [None] * (tq // LANES)
            for sub in range(tq // LANES - 1, -1, -1):
                ln = slice(sub * LANES, (sub + 1) * LANES)
                l1 = log_1m[:, ln]
                a = jnp.exp(log_beta[:, ln] + _suffix_lanes(l1, tri2) + later)
                if kb == i:
                    a = jnp.where(strict[:, ln], a, 0.0)
                parts[sub] = a.astype(BF16)
                later = later + jnp.sum(l1, axis=1, keepdims=True)
            acc = acc + jnp.dot(jnp.concatenate(parts, axis=1), vbf[cols, :],
                                preferred_element_type=F32)
        o_ref[0, rows, :] = acc


def _prompt_attention(kernel, q, k, v, extra, extra_specs, tq, name):
    b, s, _ = q.shape
    head = lambda bi, h: (bi, 0, h)
    blk = 4 * _nbytes((s, HEAD_DIM), F32) + _nbytes((s, LANES), F32) + _nbytes((N_HEAD, s), F32)
    scratch = 3 * _nbytes((s, HEAD_DIM), BF16)
    seq = pl.BlockSpec((1, s, HEAD_DIM), head)
    return pl.pallas_call(
        functools.partial(kernel, tq=tq),
        grid=(b, N_HEAD),
        in_specs=[seq, seq, seq] + extra_specs,
        out_specs=seq,
        out_shape=jax.ShapeDtypeStruct(q.shape, F32),
        scratch_shapes=[pltpu.VMEM((s, HEAD_DIM), BF16)] * 3,
        compiler_params=_params(blk, scratch, semantics=("arbitrary", "arbitrary")),
        name=name,
    )(q, k, v, *extra)


def _fox_prompt(q, k, v, c, c_t, tq=256):
    s = q.shape[1]
    specs = [pl.BlockSpec((1, s, LANES), lambda bi, h: (bi, 0, 0)),
             pl.BlockSpec((1, N_HEAD, s), lambda bi, h: (bi, 0, 0))]
    return _prompt_attention(_fox_prompt_kernel, q, k, v, (c, c_t), specs, tq, "fox_prompt")


def _sb_prompt(q, k, v, tq=256):
    return _prompt_attention(_sb_prompt_kernel, q, k, v, (), [], tq, "sb_prompt")


def _head_rows(page_ref, h, n):
    return page_ref[0, pl.ds(h, n, stride=N_HEAD), :]


def _stack_heads(rows8):
    t = 8
    return jnp.concatenate(
        [jnp.broadcast_to(rows8[h:h + 1, :], (t, LANES)) for h in range(N_HEAD)], axis=0)


def _decode_scores(q_ref, kn_ref, k_refs, page):
    t = q_ref.shape[1]
    q = q_ref[0]
    first = lax.broadcasted_iota(jnp.int32, (t, 2 * HEAD_DIM), 1) < HEAD_DIM
    qpair = []
    for pr in range(N_HEAD // 2):
        qq = q[:, pr * 2 * HEAD_DIM:(pr + 1) * 2 * HEAD_DIM]
        qpair.append(jnp.concatenate([jnp.where(first, qq, 0.0), jnp.where(first, 0.0, qq)],
                                     axis=0).astype(BF16))

    def scores(pair_keys):
        outs = [lax.dot_general(qpair[pr], pair_keys(pr), NT_DIMS, preferred_element_type=F32)
                for pr in range(N_HEAD // 2)]
        return jnp.concatenate(outs, axis=0) * SCALE

    chunks = []
    for kr in k_refs:
        chunks.append(scores(lambda pr: jnp.concatenate(
            [_head_rows(kr, 2 * pr, page), _head_rows(kr, 2 * pr + 1, page)],
            axis=1).astype(BF16)))
    kn = jnp.concatenate([kn_ref[0], jnp.zeros((page - t, W_MIX), F32)], axis=0).astype(BF16)
    chunks.append(scores(lambda pr: kn[:, pr * 2 * HEAD_DIM:(pr + 1) * 2 * HEAD_DIM]))
    return chunks


def _decode_apply(w_chunks, vn_ref, v_refs, page):
    t = vn_ref.shape[1]
    acc = [jnp.zeros((t, HEAD_DIM), F32) for _ in range(N_HEAD)]

    def apply(w, pair_values):
        for pr in range(N_HEAD // 2):
            o = jnp.dot(w[pr * 2 * t:(pr + 1) * 2 * t, :].astype(BF16), pair_values(pr),
                        preferred_element_type=F32)
            acc[2 * pr] = acc[2 * pr] + o[:t, :HEAD_DIM]
            acc[2 * pr + 1] = acc[2 * pr + 1] + o[t:, HEAD_DIM:]

    for w, vr in zip(w_chunks[:-1], v_refs):
        apply(w, lambda pr: jnp.concatenate(
            [_head_rows(vr, 2 * pr, page), _head_rows(vr, 2 * pr + 1, page)],
            axis=1).astype(BF16))
    vn = jnp.concatenate([vn_ref[0], jnp.zeros((page - t, W_MIX), F32)], axis=0).astype(BF16)
    apply(w_chunks[-1], lambda pr: vn[:, pr * 2 * HEAD_DIM:(pr + 1) * 2 * HEAD_DIM])
    return acc


def _fox_decode_kernel(pt_ref, q_ref, kn_ref, vn_ref, lfn_ref, *refs, n_pages, page):
    del pt_ref
    k_refs = refs[:n_pages]
    v_refs = refs[n_pages:2 * n_pages]
    lf_refs = refs[2 * n_pages:3 * n_pages]
    o_ref = refs[3 * n_pages]
    t = q_ref.shape[1]
    rows = N_HEAD * t

    tri3 = _tri_rhs(3, suffix=False)
    carry = jnp.zeros((N_HEAD, 1), F32)
    c_chunks = []
    for lr in lf_refs:
        loc = _prefix_lanes(lr[0], tri3)
        c_chunks.append(loc + carry)
        carry = carry + loc[:, LANES - 1:LANES]
    c_new = _prefix_lanes(lfn_ref[0], tri3) + carry
    c_chunks.append(c_new)

    sub = lax.broadcasted_iota(jnp.int32, (rows, LANES), 0) & (t - 1)
    lane = lax.broadcasted_iota(jnp.int32, (rows, LANES), 1)
    c_new_rows = _stack_heads(c_new)
    cq = jnp.sum(jnp.where(lane == sub, c_new_rows, 0.0), axis=1, keepdims=True)

    s_chunks = _decode_scores(q_ref, kn_ref, k_refs, page)
    s_chunks = [s + cq - _stack_heads(c) for s, c in zip(s_chunks, c_chunks)]
    s_chunks[-1] = jnp.where(lane <= sub, s_chunks[-1], -jnp.inf)

    mx = s_chunks[0]
    for s in s_chunks[1:]:
        mx = jnp.maximum(mx, s)
    m = jnp.max(mx, axis=1, keepdims=True)
    p_chunks = [jnp.exp(s - m) for s in s_chunks]
    tot = p_chunks[0]
    for p in p_chunks[1:]:
        tot = tot + p
    l = jnp.sum(tot, axis=1, keepdims=True)

    acc = _decode_apply(p_chunks, vn_ref, v_refs, page)
    o_ref[0] = jnp.concatenate(
        [acc[h] / l[h * t:(h + 1) * t, :] for h in range(N_HEAD)], axis=1)


def _sb_decode_kernel(pt_ref, q_ref, kn_ref, vn_ref, *refs, n_pages, page):
    del pt_ref
    k_refs = refs[:n_pages]
    v_refs = refs[n_pages:2 * n_pages]
    o_ref = refs[2 * n_pages]
    t = q_ref.shape[1]
    rows = N_HEAD * t

    sub = lax.broadcasted_iota(jnp.int32, (rows, LANES), 0) & (t - 1)
    lane = lax.broadcasted_iota(jnp.int32, (rows, LANES), 1)
    strict_new = lane < sub

    z_chunks = _decode_scores(q_ref, kn_ref, k_refs, page)
    log_beta, log_1m = [], []
    for z in z_chunks:
        lb, l1 = _log_sigmoid_pair(z)
        log_beta.append(lb)
        log_1m.append(l1)
    log_1m[-1] = jnp.where(strict_new, log_1m[-1], 0.0)

    tri2 = _tri_rhs(2, suffix=True)
    later = jnp.zeros((rows, 1), F32)
    a_chunks = [None] * len(z_chunks)
    for c in range(len(z_chunks) - 1, -1, -1):
        a = jnp.exp(log_beta[c] + _suffix_lanes(log_1m[c], tri2) + later)
        if c == len(z_chunks) - 1:
            a = jnp.where(strict_new, a, 0.0)
        a_chunks[c] = a
        later = later + jnp.sum(log_1m[c], axis=1, keepdims=True)

    acc = _decode_apply(a_chunks, vn_ref, v_refs, page)
    o_ref[0] = jnp.concatenate(acc, axis=1)


def _decode_attention(kernel, name, page_table, q, kn, vn, cache_k, cache_v, lf_new=None,
                      cache_lf=None):
    b, t, _ = q.shape
    n_pages = page_table.shape[1]
    page = cache_k.shape[1] // N_HEAD
    tok = pl.BlockSpec((1, t, W_MIX), lambda bi, pt: (bi, 0, 0))

    def paged(shape):
        return [pl.BlockSpec((1,) + shape, lambda bi, pt, p=p: (pt[bi, p], 0, 0))
                for p in range(n_pages)]

    in_specs = [tok, tok, tok]
    args = [q, kn, vn]
    blk = 4 * _nbytes((t, W_MIX), F32) + 2 * n_pages * _nbytes((page * N_HEAD, HEAD_DIM), F32)
    if cache_lf is not None:
        in_specs.append(pl.BlockSpec((1, N_HEAD, LANES), lambda bi, pt: (bi, 0, 0)))
        args.append(lf_new)
    in_specs += paged((page * N_HEAD, HEAD_DIM)) + paged((page * N_HEAD, HEAD_DIM))
    args += [cache_k] * n_pages + [cache_v] * n_pages
    if cache_lf is not None:
        in_specs += paged((N_HEAD, page))
        args += [cache_lf] * n_pages
        blk += (n_pages + 1) * _nbytes((N_HEAD, LANES), F32)
    return pl.pallas_call(
        functools.partial(kernel, n_pages=n_pages, page=page),
        grid_spec=pltpu.PrefetchScalarGridSpec(
            num_scalar_prefetch=1,
            grid=(b,),
            in_specs=in_specs,
            out_specs=pl.BlockSpec((1, t, W_MIX), lambda bi, pt: (bi, 0, 0)),
        ),
        out_shape=jax.ShapeDtypeStruct(q.shape, F32),
        compiler_params=_params(blk, semantics=("arbitrary",)),
        name=name,
    )(page_table, *args)


def _finish(o_f, o_s, x2d, g_out_fox, g_out_sb, w_out_bf, g_ffn, w_up_bf, w_down_bf, g_final):
    h, hn = _attn_out(o_f, o_s, g_out_fox, g_out_sb, x2d, w_out_bf, g_ffn)
    return _mlp(hn, w_up_bf, w_down_bf, h, g_final)


def kernel(x_prompt, x_sample, cache_fox_k, cache_fox_v, cache_fox_logf, cache_sb_k, cache_sb_v,
           page_table, g_mix, w_in, b_f, g_out_fox, g_out_sb, w_out, g_ffn, w_up, w_down, g_final):
    depth = w_in.shape[0]
    assert depth == 1, "single-layer trunk"
    b, s, d = x_prompt.shape
    db, t, _ = x_sample.shape
    n_pool, page = cache_fox_k.shape[1], cache_fox_k.shape[2]
    assert page == LANES and cache_fox_k.shape[3:] == (N_HEAD, HEAD_DIM)

    n_main = N_SEG * W_MIX
    w_in_bf = w_in[0, :, :n_main].astype(BF16)
    wg_bf = jnp.pad(w_in[0, :, n_main:], ((0, 0), (0, LANES - N_HEAD))).astype(BF16)
    bg = jnp.pad(b_f[0], (0, LANES - N_HEAD)).reshape(1, LANES).astype(F32)
    w_out_bf = w_out[0].astype(BF16)
    w_up_bf = w_up[0].astype(BF16)
    w_down_bf = w_down[0].astype(BF16)
    tail = (g_out_fox[0], g_out_sb[0], w_out_bf, g_ffn[0], w_up_bf, w_down_bf, g_final)

    xp = x_prompt.reshape(b * s, d)
    (qf, kf, vf, qs, ks, vs), lf = _project(_norm_rows(xp, g_mix[0]), w_in_bf, wg_bf, bg)
    as_seq = lambda a: a.reshape(b, s, a.shape[-1])
    c = _cumsum_seq(as_seq(lf))
    c_t = jnp.swapaxes(c[:, :, :N_HEAD], 1, 2)
    o_f = _fox_prompt(as_seq(qf), as_seq(kf), as_seq(vf), c, c_t)
    o_s = _sb_prompt(as_seq(qs), as_seq(ks), as_seq(vs))
    y_prompt = _finish(o_f.reshape(b * s, W_MIX), o_s.reshape(b * s, W_MIX), xp, *tail)

    xs = x_sample.reshape(db * t, d)
    (qf_s, kf_s, vf_s, qs_s, ks_s, vs_s), lf_s = _project(_norm_rows(xs, g_mix[0]), w_in_bf,
                                                          wg_bf, bg)
    as_tok = lambda a: a.reshape(db, t, a.shape[-1])
    rows_view = lambda cache: cache[0].reshape(n_pool, page * N_HEAD, HEAD_DIM)
    lf_cache_t = jnp.swapaxes(cache_fox_logf[0], 1, 2)
    lf_new_t = jnp.swapaxes(as_tok(lf_s)[:, :, :N_HEAD], 1, 2)
    lf_new_t = jnp.pad(lf_new_t, ((0, 0), (0, 0), (0, LANES - t)))
    o_f_s = _decode_attention(_fox_decode_kernel, "fox_decode", page_table,
                              as_tok(qf_s), as_tok(kf_s), as_tok(vf_s),
                              rows_view(cache_fox_k), rows_view(cache_fox_v),
                              lf_new=lf_new_t, cache_lf=lf_cache_t)
    o_s_s = _decode_attention(_sb_decode_kernel, "sb_decode", page_table,
                              as_tok(qs_s), as_tok(ks_s), as_tok(vs_s),
                              rows_view(cache_sb_k), rows_view(cache_sb_v))
    y_sample = _finish(o_f_s.reshape(db * t, W_MIX), o_s_s.reshape(db * t, W_MIX), xs, *tail)

    heads = lambda a, n0, n1: a.reshape(1, n0, n1, N_HEAD, HEAD_DIM)
    return (y_prompt.reshape(b, s, d), y_sample.reshape(db, t, d),
            heads(kf, b, s), heads(vf, b, s), lf[:, :N_HEAD].reshape(1, b, s, N_HEAD),
            heads(ks, b, s), heads(vs, b, s),
            heads(kf_s, db, t), heads(vf_s, db, t), lf_s[:, :N_HEAD].reshape(1, db, t, N_HEAD),
            heads(ks_s, db, t), heads(vs_s, db, t))
```

```python
import functools

import jax
import jax.numpy as jnp
from jax import lax
from jax.experimental import pallas as pl
from jax.experimental.pallas import tpu as pltpu

F32 = jnp.float32
BF16 = jnp.bfloat16

EPS = 1e-6
HEAD_DIM = 128
N_HEAD = 8
W_MIX = N_HEAD * HEAD_DIM
SCALE = HEAD_DIM ** -0.5
LANES = 128
V7X_VMEM_BYTES = 64 * 1024 * 1024
COMPILER_TEMP_BYTES = 12 * 1024 * 1024

NT_DIMS = (((1,), (1,)), ((), ()))


def _vmem_limit(block_bytes, scratch_bytes=0):
    need = 2 * block_bytes + scratch_bytes + COMPILER_TEMP_BYTES
    return int(min(need, V7X_VMEM_BYTES - 4 * 1024 * 1024))


def _nbytes(shape, dtype):
    n = 1
    for s in shape:
        n *= s
    return n * jnp.dtype(dtype).itemsize


def _params(block_bytes, scratch_bytes=0, semantics=None):
    return pltpu.CompilerParams(
        dimension_semantics=semantics,
        vmem_limit_bytes=_vmem_limit(block_bytes, scratch_bytes))


def _rms(x, g):
    ms = jnp.mean(x * x, axis=-1, keepdims=True)
    return x * lax.rsqrt(ms + EPS) * g


def _log_sigmoid_pair(z):
    log_sig = jnp.minimum(z, 0.0) - jnp.log(1.0 + jnp.exp(-jnp.abs(z)))
    return log_sig, log_sig - z


def _tri_rhs(copies, suffix):
    r = lax.broadcasted_iota(jnp.int32, (copies * LANES, LANES), 0) & (LANES - 1)
    c = lax.broadcasted_iota(jnp.int32, (copies * LANES, LANES), 1)
    cond = (r > c) if suffix else (r <= c)
    return jnp.where(cond, 1.0, 0.0).astype(BF16)


def _split3(x):
    hi = x.astype(BF16)
    r = x - hi.astype(F32)
    mid = r.astype(BF16)
    lo = (r - mid.astype(F32)).astype(BF16)
    return hi, mid, lo


def _prefix_lanes(x, tri3):
    hi, mid, lo = _split3(x)
    return jnp.dot(jnp.concatenate([hi, mid, lo], axis=1), tri3,
                   preferred_element_type=F32)


def _suffix_lanes(x, tri2):
    hi = x.astype(BF16)
    lo = (x - hi.astype(F32)).astype(BF16)
    return jnp.dot(jnp.concatenate([hi, lo], axis=1), tri2,
                   preferred_element_type=F32)


def _norm_kernel(x_ref, g_ref, o_ref):
    o_ref[...] = _rms(x_ref[...], g_ref[...]).astype(o_ref.dtype)


def _norm_rows(x, g, tm=512):
    m, d = x.shape
    blk = _nbytes((tm, d), F32) + _nbytes((tm, d), BF16)
    return pl.pallas_call(
        _norm_kernel,
        grid=(m // tm,),
        in_specs=[pl.BlockSpec((tm, d), lambda i: (i, 0)),
                  pl.BlockSpec((1, d), lambda i: (0, 0))],
        out_specs=pl.BlockSpec((tm, d), lambda i: (i, 0)),
        out_shape=jax.ShapeDtypeStruct((m, d), BF16),
        compiler_params=_params(blk, semantics=("arbitrary",)),
        name="norm_rows",
    )(x, g.reshape(1, d))


N_SEG = 6


def _proj_kernel(x_ref, w_ref, wg_ref, bg_ref, *out_refs):
    j = pl.program_id(0)
    x = x_ref[...]
    for seg in range(N_SEG):
        @pl.when(j == seg)
        def _(seg=seg):
            out_refs[seg][...] = jnp.dot(x, w_ref[...], preferred_element_type=F32)

    @pl.when(j == 0)
    def _():
        y = jnp.dot(x, wg_ref[...], preferred_element_type=F32) + bg_ref[...]
        out_refs[N_SEG][...] = _log_sigmoid_pair(y)[0]


def _project(xn, w_bf, wg_bf, bg, tm=512):
    m, d = xn.shape
    n_i = m // tm

    def rows_of(seg):
        return lambda j, i: (jnp.where(j == seg, i, jnp.where(j < seg, 0, n_i - 1)), 0)

    blk = (_nbytes((tm, d), BF16) + _nbytes((d, W_MIX), BF16) + _nbytes((d, LANES), BF16)
           + N_SEG * _nbytes((tm, W_MIX), F32) + _nbytes((tm, LANES), F32))
    outs = pl.pallas_call(
        _proj_kernel,
        grid=(N_SEG, n_i),
        in_specs=[pl.BlockSpec((tm, d), lambda j, i: (i, 0)),
                  pl.BlockSpec((d, W_MIX), lambda j, i: (0, j)),
                  pl.BlockSpec((d, LANES), lambda j, i: (0, 0)),
                  pl.BlockSpec((1, LANES), lambda j, i: (0, 0))],
        out_specs=[pl.BlockSpec((tm, W_MIX), rows_of(seg)) for seg in range(N_SEG)]
                  + [pl.BlockSpec((tm, LANES), rows_of(0))],
        out_shape=[jax.ShapeDtypeStruct((m, W_MIX), F32)] * N_SEG
                  + [jax.ShapeDtypeStruct((m, LANES), F32)],
        compiler_params=_params(blk, semantics=("arbitrary", "arbitrary")),
        name="project",
    )(xn, w_bf, wg_bf, bg)
    return outs[:N_SEG], outs[N_SEG]


def _attn_out_kernel(of_ref, os_ref, gf_ref, gs_ref, x_ref, w_ref, gn_ref, h_ref, hn_ref):
    nf = _rms(of_ref[...], gf_ref[...]).astype(BF16)
    ns = _rms(os_ref[...], gs_ref[...]).astype(BF16)
    proj = (jnp.dot(nf, w_ref[0:W_MIX, :], preferred_element_type=F32)
            + jnp.dot(ns, w_ref[W_MIX:2 * W_MIX, :], preferred_element_type=F32))
    h = x_ref[...] + proj
    h_ref[...] = h
    hn_ref[...] = _rms(h, gn_ref[...]).astype(BF16)


def _attn_out(o_f, o_s, g_f, g_s, x, w_out_bf, g_ffn, tm=512):
    m, d = x.shape
    blk = (2 * _nbytes((tm, W_MIX), F32) + 2 * _nbytes((tm, d), F32)
           + _nbytes((2 * W_MIX, d), BF16) + _nbytes((tm, d), BF16))
    row = lambda i: (i, 0)
    const = lambda i: (0, 0)
    return pl.pallas_call(
        _attn_out_kernel,
        grid=(m // tm,),
        in_specs=[pl.BlockSpec((tm, W_MIX), row), pl.BlockSpec((tm, W_MIX), row),
                  pl.BlockSpec((1, W_MIX), const), pl.BlockSpec((1, W_MIX), const),
                  pl.BlockSpec((tm, d), row),
                  pl.BlockSpec((2 * W_MIX, d), const),
                  pl.BlockSpec((1, d), const)],
        out_specs=[pl.BlockSpec((tm, d), row), pl.BlockSpec((tm, d), row)],
        out_shape=[jax.ShapeDtypeStruct((m, d), F32), jax.ShapeDtypeStruct((m, d), BF16)],
        compiler_params=_params(blk, semantics=("arbitrary",)),
        name="attn_out",
    )(o_f, o_s, g_f.reshape(1, -1), g_s.reshape(1, -1), x, w_out_bf, g_ffn.reshape(1, -1))


def _mlp_step(hn_ref, wu_ref, wd_ref, init_ref, g_ref, y_ref, *, normalize):
    f = pl.program_id(1)

    @pl.when(f == 0)
    def _():
        y_ref[...] = init_ref[...]

    u = jnp.maximum(jnp.dot(hn_ref[...], wu_ref[...], preferred_element_type=F32), 0.0)
    y_ref[...] += jnp.dot((u * u).astype(BF16), wd_ref[...], preferred_element_type=F32)

    if normalize:
        @pl.when(f == pl.num_programs(1) - 1)
        def _():
            y_ref[...] = _rms(y_ref[...], g_ref[...])


def _mlp_kernel(hn_ref, wu_ref, wd_ref, h_ref, g_ref, y_ref):
    _mlp_step(hn_ref, wu_ref, wd_ref, h_ref, g_ref, y_ref, normalize=True)


def _mlp(hn, w_up_bf, w_down_bf, h, g_final, tm=1024, tf=512):
    m, d = h.shape
    d_ff = w_up_bf.shape[1]
    blk = (_nbytes((tm, d), BF16) + _nbytes((d, tf), BF16) + _nbytes((tf, d), BF16)
           + 2 * _nbytes((tm, d), F32))
    return pl.pallas_call(
        _mlp_kernel,
        grid=(m // tm, d_ff // tf),
        in_specs=[pl.BlockSpec((tm, d), lambda i, f: (i, 0)),
                  pl.BlockSpec((d, tf), lambda i, f: (0, f)),
                  pl.BlockSpec((tf, d), lambda i, f: (f, 0)),
                  pl.BlockSpec((tm, d), lambda i, f: (i, 0)),
                  pl.BlockSpec((1, d), lambda i, f: (0, 0))],
        out_specs=pl.BlockSpec((tm, d), lambda i, f: (i, 0)),
        out_shape=jax.ShapeDtypeStruct((m, d), F32),
        compiler_params=_params(blk, semantics=("arbitrary", "arbitrary")),
        name="mlp",
    )(hn, w_up_bf, w_down_bf, h, g_final.reshape(1, d))


def _cumsum_seq_kernel(x_ref, o_ref):
    s = x_ref.shape[1]
    r = lax.broadcasted_iota(jnp.int32, (LANES, 3 * LANES), 0)
    c = lax.broadcasted_iota(jnp.int32, (LANES, 3 * LANES), 1) & (LANES - 1)
    tri3 = jnp.where(c <= r, 1.0, 0.0).astype(BF16)
    carry = jnp.zeros((1, LANES), F32)
    for ch in range(s // LANES):
        x = x_ref[0, ch * LANES:(ch + 1) * LANES, :]
        hi, mid, lo = _split3(x)
        loc = jnp.dot(tri3, jnp.concatenate([hi, mid, lo], axis=0),
                      preferred_element_type=F32)
        o_ref[0, ch * LANES:(ch + 1) * LANES, :] = loc + carry
        carry = carry + loc[LANES - 1:LANES, :]


def _cumsum_seq(x):
    b, s, _ = x.shape
    blk = 2 * _nbytes((s, LANES), F32)
    return pl.pallas_call(
        _cumsum_seq_kernel,
        grid=(b,),
        in_specs=[pl.BlockSpec((1, s, LANES), lambda i: (i, 0, 0))],
        out_specs=pl.BlockSpec((1, s, LANES), lambda i: (i, 0, 0)),
        out_shape=jax.ShapeDtypeStruct(x.shape, F32),
        compiler_params=_params(blk, semantics=("arbitrary",)),
        name="cumsum_seq",
    )(x)


def _fox_prompt_kernel(q_ref, k_ref, v_ref, c_ref, ct_ref, o_ref, qbf, kbf, vbf, *, tq):
    h = pl.program_id(1)
    s_len = q_ref.shape[1]
    qbf[...] = q_ref[0].astype(BF16)
    kbf[...] = k_ref[0].astype(BF16)
    vbf[...] = v_ref[0].astype(BF16)
    lane = lax.broadcasted_iota(jnp.int32, (tq, LANES), 1)
    qpos = lax.broadcasted_iota(jnp.int32, (tq, tq), 0)
    kpos = lax.broadcasted_iota(jnp.int32, (tq, tq), 1)
    causal = kpos <= qpos
    for i in range(s_len // tq):
        rows = slice(i * tq, (i + 1) * tq)
        q = qbf[rows, :]
        cq = jnp.sum(jnp.where(lane == h, c_ref[0, rows, :], 0.0), axis=1, keepdims=True)
        m = jnp.full((tq, 1), -jnp.inf, F32)
        l = jnp.zeros((tq, 1), F32)
        acc = jnp.zeros((tq, HEAD_DIM), F32)
        for kb in [i] + list(range(i)):
            cols = slice(kb * tq, (kb + 1) * tq)
            s = lax.dot_general(q, kbf[cols, :], NT_DIMS, preferred_element_type=F32) * SCALE
            s = s + cq - ct_ref[0, pl.ds(h, 1), cols]
            if kb == i:
                s = jnp.where(causal, s, -jnp.inf)
            m_new = jnp.maximum(m, jnp.max(s, axis=1, keepdims=True))
            alpha = jnp.exp(m - m_new)
            p = jnp.exp(s - m_new)
            l = alpha * l + jnp.sum(p, axis=1, keepdims=True)
            acc = alpha * acc + jnp.dot(p.astype(BF16), vbf[cols, :],
                                        preferred_element_type=F32)
            m = m_new
        o_ref[0, rows, :] = acc / l


def _sb_prompt_kernel(q_ref, k_ref, v_ref, o_ref, qbf, kbf, vbf, *, tq):
    s_len = q_ref.shape[1]
    qbf[...] = q_ref[0].astype(BF16)
    kbf[...] = k_ref[0].astype(BF16)
    vbf[...] = v_ref[0].astype(BF16)
    tri2 = _tri_rhs(2, suffix=True)
    qpos = lax.broadcasted_iota(jnp.int32, (tq, tq), 0)
    kpos = lax.broadcasted_iota(jnp.int32, (tq, tq), 1)
    strict = kpos < qpos
    for i in range(s_len // tq):
        rows = slice(i * tq, (i + 1) * tq)
        q = qbf[rows, :]
        later = jnp.zeros((tq, 1), F32)
        acc = jnp.zeros((tq, HEAD_DIM), F32)
        for kb in range(i, -1, -1):
            cols = slice(kb * tq, (kb + 1) * tq)
            z = lax.dot_general(q, kbf[cols, :], NT_DIMS, preferred_element_type=F32) * SCALE
            log_beta, log_1m = _log_sigmoid_pair(z)
            if kb == i:
                log_1m = jnp.where(strict, log_1m, 0.0)
            parts = [None] * (tq // LANES)
            for sub in range(tq // LANES - 1, -1, -1):
                ln = slice(sub * LANES, (sub + 1) * LANES)
                l1 = log_1m[:, ln]
                a = jnp.exp(log_beta[:, ln] + _suffix_lanes(l1, tri2) + later)
                if kb == i:
                    a = jnp.where(strict[:, ln], a, 0.0)
                parts[sub] = a.astype(BF16)
                later = later + jnp.sum(l1, axis=1, keepdims=True)
            acc = acc + jnp.dot(jnp.concatenate(parts, axis=1), vbf[cols, :],
                                preferred_element_type=F32)
        o_ref[0, rows, :] = acc


def _prompt_attention(kernel, q, k, v, extra, extra_specs, tq, name):
    b, s, _ = q.shape
    head = lambda bi, h: (bi, 0, h)
    blk = 4 * _nbytes((s, HEAD_DIM), F32) + _nbytes((s, LANES), F32) + _nbytes((N_HEAD, s), F32)
    scratch = 3 * _nbytes((s, HEAD_DIM), BF16)
    seq = pl.BlockSpec((1, s, HEAD_DIM), head)
    return pl.pallas_call(
        functools.partial(kernel, tq=tq),
        grid=(b, N_HEAD),
        in_specs=[seq, seq, seq] + extra_specs,
        out_specs=seq,
        out_shape=jax.ShapeDtypeStruct(q.shape, F32),
        scratch_shapes=[pltpu.VMEM((s, HEAD_DIM), BF16)] * 3,
        compiler_params=_params(blk, scratch, semantics=("arbitrary", "arbitrary")),
        name=name,
    )(q, k, v, *extra)


def _fox_prompt(q, k, v, c, c_t, tq=256):
    s = q.shape[1]
    specs = [pl.BlockSpec((1, s, LANES), lambda bi, h: (bi, 0, 0)),
             pl.BlockSpec((1, N_HEAD, s), lambda bi, h: (bi, 0, 0))]
    return _prompt_attention(_fox_prompt_kernel, q, k, v, (c, c_t), specs, tq, "fox_prompt")


def _sb_prompt(q, k, v, tq=256):
    return _prompt_attention(_sb_prompt_kernel, q, k, v, (), [], tq, "sb_prompt")


def _head_rows(page_ref, h, n):
    return page_ref[0, pl.ds(h, n, stride=N_HEAD), :]


def _stack_heads(rows8):
    t = 8
    return jnp.concatenate(
        [jnp.broadcast_to(rows8[h:h + 1, :], (t, LANES)) for h in range(N_HEAD)], axis=0)


def _decode_scores(q_ref, kn_ref, k_refs, page):
    t = q_ref.shape[1]
    q = q_ref[0]
    first = lax.broadcasted_iota(jnp.int32, (t, 2 * HEAD_DIM), 1) < HEAD_DIM
    qpair = []
    for pr in range(N_HEAD // 2):
        qq = q[:, pr * 2 * HEAD_DIM:(pr + 1) * 2 * HEAD_DIM]
        qpair.append(jnp.concatenate([jnp.where(first, qq, 0.0), jnp.where(first, 0.0, qq)],
                                     axis=0).astype(BF16))

    def scores(pair_keys):
        outs = [lax.dot_general(qpair[pr], pair_keys(pr), NT_DIMS, preferred_element_type=F32)
                for pr in range(N_HEAD // 2)]
        return jnp.concatenate(outs, axis=0) * SCALE

    chunks = []
    for kr in k_refs:
        chunks.append(scores(lambda pr: jnp.concatenate(
            [_head_rows(kr, 2 * pr, page), _head_rows(kr, 2 * pr + 1, page)],
            axis=1).astype(BF16)))
    kn = jnp.concatenate([kn_ref[0], jnp.zeros((page - t, W_MIX), F32)], axis=0).astype(BF16)
    chunks.append(scores(lambda pr: kn[:, pr * 2 * HEAD_DIM:(pr + 1) * 2 * HEAD_DIM]))
    return chunks


def _decode_apply(w_chunks, vn_ref, v_refs, page):
    t = vn_ref.shape[1]
    acc = [jnp.zeros((t, HEAD_DIM), F32) for _ in range(N_HEAD)]

    def apply(w, pair_values):
        for pr in range(N_HEAD // 2):
            o = jnp.dot(w[pr * 2 * t:(pr + 1) * 2 * t, :].astype(BF16), pair_values(pr),
                        preferred_element_type=F32)
            acc[2 * pr] = acc[2 * pr] + o[:t, :HEAD_DIM]
            acc[2 * pr + 1] = acc[2 * pr + 1] + o[t:, HEAD_DIM:]

    for w, vr in zip(w_chunks[:-1], v_refs):
        apply(w, lambda pr: jnp.concatenate(
            [_head_rows(vr, 2 * pr, page), _head_rows(vr, 2 * pr + 1, page)],
            axis=1).astype(BF16))
    vn = jnp.concatenate([vn_ref[0], jnp.zeros((page - t, W_MIX), F32)], axis=0).astype(BF16)
    apply(w_chunks[-1], lambda pr: vn[:, pr * 2 * HEAD_DIM:(pr + 1) * 2 * HEAD_DIM])
    return acc


def _fox_decode(in_refs, o_ref, *, n_pages, page):
    q_ref, kn_ref, vn_ref, lfn_ref = in_refs[:4]
    k_refs = in_refs[4:4 + n_pages]
    v_refs = in_refs[4 + n_pages:4 + 2 * n_pages]
    lf_refs = in_refs[4 + 2 * n_pages:4 + 3 * n_pages]
    t = q_ref.shape[1]
    rows = N_HEAD * t

    tri3 = _tri_rhs(3, suffix=False)
    carry = jnp.zeros((N_HEAD, 1), F32)
    c_chunks = []
    for lr in lf_refs:
        loc = _prefix_lanes(lr[0], tri3)
        c_chunks.append(loc + carry)
        carry = carry + loc[:, LANES - 1:LANES]
    c_new = _prefix_lanes(lfn_ref[0], tri3) + carry
    c_chunks.append(c_new)

    sub = lax.broadcasted_iota(jnp.int32, (rows, LANES), 0) & (t - 1)
    lane = lax.broadcasted_iota(jnp.int32, (rows, LANES), 1)
    c_new_rows = _stack_heads(c_new)
    cq = jnp.sum(jnp.where(lane == sub, c_new_rows, 0.0), axis=1, keepdims=True)

    s_chunks = _decode_scores(q_ref, kn_ref, k_refs, page)
    s_chunks = [s + cq - _stack_heads(c) for s, c in zip(s_chunks, c_chunks)]
    s_chunks[-1] = jnp.where(lane <= sub, s_chunks[-1], -jnp.inf)

    mx = s_chunks[0]
    for s in s_chunks[1:]:
        mx = jnp.maximum(mx, s)
    m = jnp.max(mx, axis=1, keepdims=True)
    p_chunks = [jnp.exp(s - m) for s in s_chunks]
    tot = p_chunks[0]
    for p in p_chunks[1:]:
        tot = tot + p
    l = jnp.sum(tot, axis=1, keepdims=True)

    acc = _decode_apply(p_chunks, vn_ref, v_refs, page)
    o_ref[0] = jnp.concatenate(
        [acc[h] / l[h * t:(h + 1) * t, :] for h in range(N_HEAD)], axis=1)


def _sb_decode(in_refs, o_ref, *, n_pages, page):
    q_ref, kn_ref, vn_ref = in_refs[:3]
    k_refs = in_refs[3:3 + n_pages]
    v_refs = in_refs[3 + n_pages:3 + 2 * n_pages]
    t = q_ref.shape[1]
    rows = N_HEAD * t

    sub = lax.broadcasted_iota(jnp.int32, (rows, LANES), 0) & (t - 1)
    lane = lax.broadcasted_iota(jnp.int32, (rows, LANES), 1)
    strict_new = lane < sub

    z_chunks = _decode_scores(q_ref, kn_ref, k_refs, page)
    log_beta, log_1m = [], []
    for z in z_chunks:
        lb, l1 = _log_sigmoid_pair(z)
        log_beta.append(lb)
        log_1m.append(l1)
    log_1m[-1] = jnp.where(strict_new, log_1m[-1], 0.0)

    tri2 = _tri_rhs(2, suffix=True)
    later = jnp.zeros((rows, 1), F32)
    a_chunks = [None] * len(z_chunks)
    for c in range(len(z_chunks) - 1, -1, -1):
        a = jnp.exp(log_beta[c] + _suffix_lanes(log_1m[c], tri2) + later)
        if c == len(z_chunks) - 1:
            a = jnp.where(strict_new, a, 0.0)
        a_chunks[c] = a
        later = later + jnp.sum(log_1m[c], axis=1, keepdims=True)

    acc = _decode_apply(a_chunks, vn_ref, v_refs, page)
    o_ref[0] = jnp.concatenate(acc, axis=1)


def _mlp_decode_kernel(pt_ref, hn_ref, wu_ref, wd_ref, init_ref, g_ref, *refs, decode, normalize):
    del pt_ref
    *dec_refs, y_ref, o_ref = refs
    _mlp_step(hn_ref, wu_ref, wd_ref, init_ref, g_ref, y_ref, normalize=normalize)
    decode(dec_refs, o_ref)


def _mlp_with_decode(name, hn, w_up_bf, w_down_bf, init, g_final, f_lo, n_f, normalize,
                     page_table, q, kn, vn, cache_k, cache_v, lf_new=None, cache_lf=None,
                     tm=512, tf=512):
    m, d = init.shape
    b, t, _ = q.shape
    n_i = m // tm
    assert n_i * n_f == b, "one decode sequence per MLP grid step"
    n_pages = page_table.shape[1]
    page = cache_k.shape[1] // N_HEAD
    seq = lambda i, f: i * n_f + f
    tok = pl.BlockSpec((1, t, W_MIX), lambda i, f, pt: (seq(i, f), 0, 0))

    def paged(shape):
        return [pl.BlockSpec((1,) + shape, lambda i, f, pt, p=p: (pt[seq(i, f), p], 0, 0))
                for p in range(n_pages)]

    in_specs = [pl.BlockSpec((tm, d), lambda i, f, pt: (i, 0)),
                pl.BlockSpec((d, tf), lambda i, f, pt: (0, f_lo + f)),
                pl.BlockSpec((tf, d), lambda i, f, pt: (f_lo + f, 0)),
                pl.BlockSpec((tm, d), lambda i, f, pt: (i, 0), pipeline_mode=pl.Buffered(1)),
                pl.BlockSpec((1, d), lambda i, f, pt: (0, 0)),
                tok, tok, tok]
    args = [hn, w_up_bf, w_down_bf, init, g_final.reshape(1, d), q, kn, vn]
    double = (_nbytes((tm, d), BF16) + _nbytes((d, tf), BF16) + _nbytes((tf, d), BF16)
              + _nbytes((tm, d), F32) + 4 * _nbytes((t, W_MIX), F32)
              + 2 * n_pages * _nbytes((page * N_HEAD, HEAD_DIM), F32))
    single = _nbytes((tm, d), F32)
    if cache_lf is None:
        decode = _sb_decode
    else:
        decode = _fox_decode
        in_specs.append(pl.BlockSpec((1, N_HEAD, LANES), lambda i, f, pt: (seq(i, f), 0, 0)))
        args.append(lf_new)
    in_specs += paged((page * N_HEAD, HEAD_DIM)) + paged((page * N_HEAD, HEAD_DIM))
    args += [cache_k] * n_pages + [cache_v] * n_pages
    if cache_lf is not None:
        in_specs += paged((N_HEAD, page))
        args += [cache_lf] * n_pages
        double += (n_pages + 1) * _nbytes((N_HEAD, LANES), F32)
    return pl.pallas_call(
        functools.partial(_mlp_decode_kernel, normalize=normalize,
                          decode=functools.partial(decode, n_pages=n_pages, page=page)),
        grid_spec=pltpu.PrefetchScalarGridSpec(
            num_scalar_prefetch=1,
            grid=(n_i, n_f),
            in_specs=in_specs,
            out_specs=[pl.BlockSpec((tm, d), lambda i, f, pt: (i, 0)),
                       pl.BlockSpec((1, t, W_MIX), lambda i, f, pt: (seq(i, f), 0, 0))],
        ),
        out_shape=[jax.ShapeDtypeStruct((m, d), F32), jax.ShapeDtypeStruct(q.shape, F32)],
        compiler_params=_params(double, single, semantics=("arbitrary", "arbitrary")),
        name=name,
    )(page_table, *args)


MLP_TILE = 512


def kernel(x_prompt, x_sample, cache_fox_k, cache_fox_v, cache_fox_logf, cache_sb_k, cache_sb_v,
           page_table, g_mix, w_in, b_f, g_out_fox, g_out_sb, w_out, g_ffn, w_up, w_down, g_final):
    depth = w_in.shape[0]
    assert depth == 1, "single-layer trunk"
    b, s, d = x_prompt.shape
    db, t, _ = x_sample.shape
    n_pool, page = cache_fox_k.shape[1], cache_fox_k.shape[2]
    assert page == LANES and cache_fox_k.shape[3:] == (N_HEAD, HEAD_DIM)

    n_main = N_SEG * W_MIX
    w_in_bf = w_in[0, :, :n_main].astype(BF16)
    wg_bf = jnp.pad(w_in[0, :, n_main:], ((0, 0), (0, LANES - N_HEAD))).astype(BF16)
    bg = jnp.pad(b_f[0], (0, LANES - N_HEAD)).reshape(1, LANES).astype(F32)
    w_out_bf = w_out[0].astype(BF16)
    w_up_bf = w_up[0].astype(BF16)
    w_down_bf = w_down[0].astype(BF16)
    mix_out = (g_out_fox[0], g_out_sb[0])

    xp = x_prompt.reshape(b * s, d)
    (qf, kf, vf, qs, ks, vs), lf = _project(_norm_rows(xp, g_mix[0]), w_in_bf, wg_bf, bg)
    xs = x_sample.reshape(db * t, d)
    (qf_s, kf_s, vf_s, qs_s, ks_s, vs_s), lf_s = _project(_norm_rows(xs, g_mix[0]), w_in_bf,
                                                          wg_bf, bg)

    as_seq = lambda a: a.reshape(b, s, a.shape[-1])
    c = _cumsum_seq(as_seq(lf))
    c_t = jnp.swapaxes(c[:, :, :N_HEAD], 1, 2)
    o_f = _fox_prompt(as_seq(qf), as_seq(kf), as_seq(vf), c, c_t)
    o_s = _sb_prompt(as_seq(qs), as_seq(ks), as_seq(vs))
    h, hn = _attn_out(o_f.reshape(b * s, W_MIX), o_s.reshape(b * s, W_MIX), *mix_out, xp,
                      w_out_bf, g_ffn[0])

    as_tok = lambda a: a.reshape(db, t, a.shape[-1])
    rows_view = lambda cache: cache[0].reshape(n_pool, page * N_HEAD, HEAD_DIM)
    lf_cache_t = jnp.swapaxes(cache_fox_logf[0], 1, 2)
    lf_new_t = jnp.swapaxes(as_tok(lf_s)[:, :, :N_HEAD], 1, 2)
    lf_new_t = jnp.pad(lf_new_t, ((0, 0), (0, 0), (0, LANES - t)))
    n_f = db // (b * s // MLP_TILE)
    assert 2 * n_f * MLP_TILE == w_up_bf.shape[1], "two halves cover the hidden width"
    mlp_in = (hn, w_up_bf, w_down_bf)
    y_half, o_f_s = _mlp_with_decode(
        "mlp_fox_decode", *mlp_in, h, g_final, 0, n_f, False, page_table,
        as_tok(qf_s), as_tok(kf_s), as_tok(vf_s), rows_view(cache_fox_k),
        rows_view(cache_fox_v), lf_new=lf_new_t, cache_lf=lf_cache_t,
        tm=MLP_TILE, tf=MLP_TILE)
    y_prompt, o_s_s = _mlp_with_decode(
        "mlp_sb_decode", *mlp_in, y_half, g_final, n_f, n_f, True, page_table,
        as_tok(qs_s), as_tok(ks_s), as_tok(vs_s), rows_view(cache_sb_k),
        rows_view(cache_sb_v), tm=MLP_TILE, tf=MLP_TILE)

    h_s, hn_s = _attn_out(o_f_s.reshape(db * t, W_MIX), o_s_s.reshape(db * t, W_MIX), *mix_out,
                          xs, w_out_bf, g_ffn[0])
    y_sample = _mlp(hn_s, w_up_bf, w_down_bf, h_s, g_final)

    heads = lambda a, n0, n1: a.reshape(1, n0, n1, N_HEAD, HEAD_DIM)
    return (y_prompt.reshape(b, s, d), y_sample.reshape(db, t, d),
            heads(kf, b, s), heads(vf, b, s), lf[:, :N_HEAD].reshape(1, b, s, N_HEAD),
            heads(ks, b, s), heads(vs, b, s),
            heads(kf_s, db, t), heads(vf_s, db, t), lf_s[:, :N_HEAD].reshape(1, db, t, N_HEAD),
            heads(ks_s, db, t), heads(vs_s, db, t))
```

```python
import functools

import jax
import jax.numpy as jnp
from jax import lax
from jax.experimental import pallas as pl
from jax.experimental.pallas import tpu as pltpu

F32 = jnp.float32
BF16 = jnp.bfloat16

EPS = 1e-6
HEAD_DIM = 128
N_HEAD = 8
W_MIX = N_HEAD * HEAD_DIM
SCALE = HEAD_DIM ** -0.5
LANES = 128
V7X_VMEM_BYTES = 64 * 1024 * 1024
COMPILER_TEMP_BYTES = 12 * 1024 * 1024

NT_DIMS = (((1,), (1,)), ((), ()))


def _vmem_limit(block_bytes, scratch_bytes=0):
    need = 2 * block_bytes + scratch_bytes + COMPILER_TEMP_BYTES
    return int(min(need, V7X_VMEM_BYTES - 4 * 1024 * 1024))


def _nbytes(shape, dtype):
    n = 1
    for s in shape:
        n *= s
    return n * jnp.dtype(dtype).itemsize


def _params(block_bytes, scratch_bytes=0, semantics=None):
    return pltpu.CompilerParams(
        dimension_semantics=semantics,
        vmem_limit_bytes=_vmem_limit(block_bytes, scratch_bytes))


def _rms(x, g):
    ms = jnp.mean(x * x, axis=-1, keepdims=True)
    return x * lax.rsqrt(ms + EPS) * g


def _log_sigmoid_pair(z):
    log_sig = jnp.minimum(z, 0.0) - jnp.log(1.0 + jnp.exp(-jnp.abs(z)))
    return log_sig, log_sig - z


def _tri_rhs(copies, suffix):
    r = lax.broadcasted_iota(jnp.int32, (copies * LANES, LANES), 0) & (LANES - 1)
    c = lax.broadcasted_iota(jnp.int32, (copies * LANES, LANES), 1)
    cond = (r > c) if suffix else (r <= c)
    return jnp.where(cond, 1.0, 0.0).astype(BF16)


def _split3(x):
    hi = x.astype(BF16)
    r = x - hi.astype(F32)
    mid = r.astype(BF16)
    lo = (r - mid.astype(F32)).astype(BF16)
    return hi, mid, lo


def _prefix_lanes(x, tri3):
    hi, mid, lo = _split3(x)
    return jnp.dot(jnp.concatenate([hi, mid, lo], axis=1), tri3,
                   preferred_element_type=F32)


def _suffix_lanes(x, tri2):
    hi = x.astype(BF16)
    lo = (x - hi.astype(F32)).astype(BF16)
    return jnp.dot(jnp.concatenate([hi, lo], axis=1), tri2,
                   preferred_element_type=F32)


def _norm_kernel(x_ref, g_ref, o_ref):
    o_ref[...] = _rms(x_ref[...], g_ref[...]).astype(o_ref.dtype)


def _norm_rows(x, g, tm=512):
    m, d = x.shape
    blk = _nbytes((tm, d), F32) + _nbytes((tm, d), BF16)
    return pl.pallas_call(
        _norm_kernel,
        grid=(m // tm,),
        in_specs=[pl.BlockSpec((tm, d), lambda i: (i, 0)),
                  pl.BlockSpec((1, d), lambda i: (0, 0))],
        out_specs=pl.BlockSpec((tm, d), lambda i: (i, 0)),
        out_shape=jax.ShapeDtypeStruct((m, d), BF16),
        compiler_params=_params(blk, semantics=("arbitrary",)),
        name="norm_rows",
    )(x, g.reshape(1, d))


N_SEG = 6


def _proj_kernel(x_ref, w_ref, wg_ref, bg_ref, *out_refs):
    j = pl.program_id(0)
    x = x_ref[...]
    for seg in range(N_SEG):
        @pl.when(j == seg)
        def _(seg=seg):
            out_refs[seg][...] = jnp.dot(x, w_ref[...], preferred_element_type=F32)

    @pl.when(j == 0)
    def _():
        y = jnp.dot(x, wg_ref[...], preferred_element_type=F32) + bg_ref[...]
        out_refs[N_SEG][...] = _log_sigmoid_pair(y)[0]


def _project(xn, w_bf, wg_bf, bg, tm=512):
    m, d = xn.shape
    n_i = m // tm

    def rows_of(seg):
        return lambda j, i: (jnp.where(j == seg, i, jnp.where(j < seg, 0, n_i - 1)), 0)

    blk = (_nbytes((tm, d), BF16) + _nbytes((d, W_MIX), BF16) + _nbytes((d, LANES), BF16)
           + N_SEG * _nbytes((tm, W_MIX), F32) + _nbytes((tm, LANES), F32))
    outs = pl.pallas_call(
        _proj_kernel,
        grid=(N_SEG, n_i),
        in_specs=[pl.BlockSpec((tm, d), lambda j, i: (i, 0)),
                  pl.BlockSpec((d, W_MIX), lambda j, i: (0, j)),
                  pl.BlockSpec((d, LANES), lambda j, i: (0, 0)),
                  pl.BlockSpec((1, LANES), lambda j, i: (0, 0))],
        out_specs=[pl.BlockSpec((tm, W_MIX), rows_of(seg)) for seg in range(N_SEG)]
                  + [pl.BlockSpec((tm, LANES), rows_of(0))],
        out_shape=[jax.ShapeDtypeStruct((m, W_MIX), F32)] * N_SEG
                  + [jax.ShapeDtypeStruct((m, LANES), F32)],
        compiler_params=_params(blk, semantics=("arbitrary", "arbitrary")),
        name="project",
    )(xn, w_bf, wg_bf, bg)
    return outs[:N_SEG], outs[N_SEG]


def _attn_out_kernel(of_ref, os_ref, gf_ref, gs_ref, x_ref, w_ref, gn_ref, h_ref, hn_ref):
    nf = _rms(of_ref[...], gf_ref[...]).astype(BF16)
    ns = _rms(os_ref[...], gs_ref[...]).astype(BF16)
    proj = (jnp.dot(nf, w_ref[0:W_MIX, :], preferred_element_type=F32)
            + jnp.dot(ns, w_ref[W_MIX:2 * W_MIX, :], preferred_element_type=F32))
    h = x_ref[...] + proj
    h_ref[...] = h
    hn_ref[...] = _rms(h, gn_ref[...]).astype(BF16)


def _attn_out(o_f, o_s, g_f, g_s, x, w_out_bf, g_ffn, tm=512):
    m, d = x.shape
    blk = (2 * _nbytes((tm, W_MIX), F32) + 2 * _nbytes((tm, d), F32)
           + _nbytes((2 * W_MIX, d), BF16) + _nbytes((tm, d), BF16))
    row = lambda i: (i, 0)
    const = lambda i: (0, 0)
    return pl.pallas_call(
        _attn_out_kernel,
        grid=(m // tm,),
        in_specs=[pl.BlockSpec((tm, W_MIX), row), pl.BlockSpec((tm, W_MIX), row),
                  pl.BlockSpec((1, W_MIX), const), pl.BlockSpec((1, W_MIX), const),
                  pl.BlockSpec((tm, d), row),
                  pl.BlockSpec((2 * W_MIX, d), const),
                  pl.BlockSpec((1, d), const)],
        out_specs=[pl.BlockSpec((tm, d), row), pl.BlockSpec((tm, d), row)],
        out_shape=[jax.ShapeDtypeStruct((m, d), F32), jax.ShapeDtypeStruct((m, d), BF16)],
        compiler_params=_params(blk, semantics=("arbitrary",)),
        name="attn_out",
    )(o_f, o_s, g_f.reshape(1, -1), g_s.reshape(1, -1), x, w_out_bf, g_ffn.reshape(1, -1))


def _mlp_step(hn_ref, wu_ref, wd_ref, init_ref, g_ref, y_ref, *, normalize):
    f = pl.program_id(1)

    @pl.when(f == 0)
    def _():
        y_ref[...] = init_ref[...]

    u = jnp.maximum(jnp.dot(hn_ref[...], wu_ref[...], preferred_element_type=F32), 0.0)
    y_ref[...] += jnp.dot((u * u).astype(BF16), wd_ref[...], preferred_element_type=F32)

    if normalize:
        @pl.when(f == pl.num_programs(1) - 1)
        def _():
            y_ref[...] = _rms(y_ref[...], g_ref[...])


def _mlp_kernel(hn_ref, wu_ref, wd_ref, h_ref, g_ref, y_ref):
    _mlp_step(hn_ref, wu_ref, wd_ref, h_ref, g_ref, y_ref, normalize=True)


def _mlp(hn, w_up_bf, w_down_bf, h, g_final, tm=1024, tf=512):
    m, d = h.shape
    d_ff = w_up_bf.shape[1]
    blk = (_nbytes((tm, d), BF16) + _nbytes((d, tf), BF16) + _nbytes((tf, d), BF16)
           + 2 * _nbytes((tm, d), F32))
    return pl.pallas_call(
        _mlp_kernel,
        grid=(m // tm, d_ff // tf),
        in_specs=[pl.BlockSpec((tm, d), lambda i, f: (i, 0)),
                  pl.BlockSpec((d, tf), lambda i, f: (0, f)),
                  pl.BlockSpec((tf, d), lambda i, f: (f, 0)),
                  pl.BlockSpec((tm, d), lambda i, f: (i, 0)),
                  pl.BlockSpec((1, d), lambda i, f: (0, 0))],
        out_specs=pl.BlockSpec((tm, d), lambda i, f: (i, 0)),
        out_shape=jax.ShapeDtypeStruct((m, d), F32),
        compiler_params=_params(blk, semantics=("arbitrary", "arbitrary")),
        name="mlp",
    )(hn, w_up_bf, w_down_bf, h, g_final.reshape(1, d))


def _cumsum_seq_kernel(x_ref, o_ref):
    s = x_ref.shape[1]
    r = lax.broadcasted_iota(jnp.int32, (LANES, 3 * LANES), 0)
    c = lax.broadcasted_iota(jnp.int32, (LANES, 3 * LANES), 1) & (LANES - 1)
    tri3 = jnp.where(c <= r, 1.0, 0.0).astype(BF16)
    carry = jnp.zeros((1, LANES), F32)
    for ch in range(s // LANES):
        x = x_ref[0, ch * LANES:(ch + 1) * LANES, :]
        hi, mid, lo = _split3(x)
        loc = jnp.dot(tri3, jnp.concatenate([hi, mid, lo], axis=0),
                      preferred_element_type=F32)
        o_ref[0, ch * LANES:(ch + 1) * LANES, :] = loc + carry
        carry = carry + loc[LANES - 1:LANES, :]


def _cumsum_seq(x):
    b, s, _ = x.shape
    blk = 2 * _nbytes((s, LANES), F32)
    return pl.pallas_call(
        _cumsum_seq_kernel,
        grid=(b,),
        in_specs=[pl.BlockSpec((1, s, LANES), lambda i: (i, 0, 0))],
        out_specs=pl.BlockSpec((1, s, LANES), lambda i: (i, 0, 0)),
        out_shape=jax.ShapeDtypeStruct(x.shape, F32),
        compiler_params=_params(blk, semantics=("arbitrary",)),
        name="cumsum_seq",
    )(x)


def _fox_prompt_kernel(q_ref, k_ref, v_ref, c_ref, ct_ref, o_ref, qbf, kbf, vbf, *, tq):
    h = pl.program_id(1)
    s_len = q_ref.shape[1]
    qbf[...] = q_ref[0].astype(BF16)
    kbf[...] = k_ref[0].astype(BF16)
    vbf[...] = v_ref[0].astype(BF16)
    lane = lax.broadcasted_iota(jnp.int32, (tq, LANES), 1)
    qpos = lax.broadcasted_iota(jnp.int32, (tq, tq), 0)
    kpos = lax.broadcasted_iota(jnp.int32, (tq, tq), 1)
    causal = kpos <= qpos
    for i in range(s_len // tq):
        rows = slice(i * tq, (i + 1) * tq)
        q = qbf[rows, :]
        cq = jnp.sum(jnp.where(lane == h, c_ref[0, rows, :], 0.0), axis=1, keepdims=True)
        m = jnp.full((tq, 1), -jnp.inf, F32)
        l = jnp.zeros((tq, 1), F32)
        acc = jnp.zeros((tq, HEAD_DIM), F32)
        for kb in [i] + list(range(i)):
            cols = slice(kb * tq, (kb + 1) * tq)
            s = lax.dot_general(q, kbf[cols, :], NT_DIMS, preferred_element_type=F32) * SCALE
            s = s + cq - ct_ref[0, pl.ds(h, 1), cols]
            if kb == i:
                s = jnp.where(causal, s, -jnp.inf)
            m_new = jnp.maximum(m, jnp.max(s, axis=1, keepdims=True))
            alpha = jnp.exp(m - m_new)
            p = jnp.exp(s - m_new)
            l = alpha * l + jnp.sum(p, axis=1, keepdims=True)
            acc = alpha * acc + jnp.dot(p.astype(BF16), vbf[cols, :],
                                        preferred_element_type=F32)
            m = m_new
        o_ref[0, rows, :] = acc / l


def _sb_prompt_kernel(q_ref, k_ref, v_ref, o_ref, qbf, kbf, vbf, *, tq):
    s_len = q_ref.shape[1]
    qbf[...] = q_ref[0].astype(BF16)
    kbf[...] = k_ref[0].astype(BF16)
    vbf[...] = v_ref[0].astype(BF16)
    tri2 = _tri_rhs(2, suffix=True)
    qpos = lax.broadcasted_iota(jnp.int32, (tq, tq), 0)
    kpos = lax.broadcasted_iota(jnp.int32, (tq, tq), 1)
    strict = kpos < qpos
    for i in range(s_len // tq):
        rows = slice(i * tq, (i + 1) * tq)
        q = qbf[rows, :]
        later = jnp.zeros((tq, 1), F32)
        acc = jnp.zeros((tq, HEAD_DIM), F32)
        for kb in range(i, -1, -1):
            cols = slice(kb * tq, (kb + 1) * tq)
            z = lax.dot_general(q, kbf[cols, :], NT_DIMS, preferred_element_type=F32) * SCALE
            log_beta, log_1m = _log_sigmoid_pair(z)
            if kb == i:
                log_1m = jnp.where(strict, log_1m, 0.0)
            parts = [None] * (tq // LANES)
            for sub in range(tq // LANES - 1, -1, -1):
                ln = slice(sub * LANES, (sub + 1) * LANES)
                l1 = log_1m[:, ln]
                a = jnp.exp(log_beta[:, ln] + _suffix_lanes(l1, tri2) + later)
                if kb == i:
                    a = jnp.where(strict[:, ln], a, 0.0)
                parts[sub] = a.astype(BF16)
                later = later + jnp.sum(l1, axis=1, keepdims=True)
            acc = acc + jnp.dot(jnp.concatenate(parts, axis=1), vbf[cols, :],
                                preferred_element_type=F32)
        o_ref[0, rows, :] = acc


def _prompt_attention(kernel, q, k, v, extra, extra_specs, tq, name):
    b, s, _ = q.shape
    head = lambda bi, h: (bi, 0, h)
    blk = 4 * _nbytes((s, HEAD_DIM), F32) + _nbytes((s, LANES), F32) + _nbytes((N_HEAD, s), F32)
    scratch = 3 * _nbytes((s, HEAD_DIM), BF16)
    seq = pl.BlockSpec((1, s, HEAD_DIM), head)
    return pl.pallas_call(
        functools.partial(kernel, tq=tq),
        grid=(b, N_HEAD),
        in_specs=[seq, seq, seq] + extra_specs,
        out_specs=seq,
        out_shape=jax.ShapeDtypeStruct(q.shape, F32),
        scratch_shapes=[pltpu.VMEM((s, HEAD_DIM), BF16)] * 3,
        compiler_params=_params(blk, scratch, semantics=("arbitrary", "arbitrary")),
        name=name,
    )(q, k, v, *extra)


def _fox_prompt(q, k, v, c, c_t, tq=256):
    s = q.shape[1]
    specs = [pl.BlockSpec((1, s, LANES), lambda bi, h: (bi, 0, 0)),
             pl.BlockSpec((1, N_HEAD, s), lambda bi, h: (bi, 0, 0))]
    return _prompt_attention(_fox_prompt_kernel, q, k, v, (c, c_t), specs, tq, "fox_prompt")


def _sb_prompt(q, k, v, tq=256):
    return _prompt_attention(_sb_prompt_kernel, q, k, v, (), [], tq, "sb_prompt")


def _head_rows(page_ref, h, n):
    return page_ref[pl.ds(h, n, stride=N_HEAD), :]


def _stack_heads(rows8):
    t = 8
    return jnp.concatenate(
        [jnp.broadcast_to(rows8[h:h + 1, :], (t, LANES)) for h in range(N_HEAD)], axis=0)


def _decode_scores(q_ref, kn_ref, k_refs, page):
    t = q_ref.shape[1]
    q = q_ref[0]
    first = lax.broadcasted_iota(jnp.int32, (t, 2 * HEAD_DIM), 1) < HEAD_DIM
    qpair = []
    for pr in range(N_HEAD // 2):
        qq = q[:, pr * 2 * HEAD_DIM:(pr + 1) * 2 * HEAD_DIM]
        qpair.append(jnp.concatenate([jnp.where(first, qq, 0.0), jnp.where(first, 0.0, qq)],
                                     axis=0).astype(BF16))

    def scores(pair_keys):
        outs = [lax.dot_general(qpair[pr], pair_keys(pr), NT_DIMS, preferred_element_type=F32)
                for pr in range(N_HEAD // 2)]
        return jnp.concatenate(outs, axis=0) * SCALE

    chunks = []
    for kr in k_refs:
        chunks.append(scores(lambda pr: jnp.concatenate(
            [_head_rows(kr, 2 * pr, page), _head_rows(kr, 2 * pr + 1, page)],
            axis=1).astype(BF16)))
    kn = jnp.concatenate([kn_ref[0], jnp.zeros((page - t, W_MIX), F32)], axis=0).astype(BF16)
    chunks.append(scores(lambda pr: kn[:, pr * 2 * HEAD_DIM:(pr + 1) * 2 * HEAD_DIM]))
    return chunks


def _decode_apply(w_chunks, vn_ref, v_refs, page):
    t = vn_ref.shape[1]
    acc = [jnp.zeros((t, HEAD_DIM), F32) for _ in range(N_HEAD)]

    def apply(w, pair_values):
        for pr in range(N_HEAD // 2):
            o = jnp.dot(w[pr * 2 * t:(pr + 1) * 2 * t, :].astype(BF16), pair_values(pr),
                        preferred_element_type=F32)
            acc[2 * pr] = acc[2 * pr] + o[:t, :HEAD_DIM]
            acc[2 * pr + 1] = acc[2 * pr + 1] + o[t:, HEAD_DIM:]

    for w, vr in zip(w_chunks[:-1], v_refs):
        apply(w, lambda pr: jnp.concatenate(
            [_head_rows(vr, 2 * pr, page), _head_rows(vr, 2 * pr + 1, page)],
            axis=1).astype(BF16))
    vn = jnp.concatenate([vn_ref[0], jnp.zeros((page - t, W_MIX), F32)], axis=0).astype(BF16)
    apply(w_chunks[-1], lambda pr: vn[:, pr * 2 * HEAD_DIM:(pr + 1) * 2 * HEAD_DIM])
    return acc


def _fox_decode(tok_refs, page_refs, o_ref, *, page):
    q_ref, kn_ref, vn_ref, lfn_ref = tok_refs
    k_refs, v_refs, lf_refs = page_refs
    t = q_ref.shape[1]
    rows = N_HEAD * t

    tri3 = _tri_rhs(3, suffix=False)
    carry = jnp.zeros((N_HEAD, 1), F32)
    c_chunks = []
    for lr in lf_refs:
        loc = _prefix_lanes(lr[...], tri3)
        c_chunks.append(loc + carry)
        carry = carry + loc[:, LANES - 1:LANES]
    c_new = _prefix_lanes(lfn_ref[0], tri3) + carry
    c_chunks.append(c_new)

    sub = lax.broadcasted_iota(jnp.int32, (rows, LANES), 0) & (t - 1)
    lane = lax.broadcasted_iota(jnp.int32, (rows, LANES), 1)
    c_new_rows = _stack_heads(c_new)
    cq = jnp.sum(jnp.where(lane == sub, c_new_rows, 0.0), axis=1, keepdims=True)

    s_chunks = _decode_scores(q_ref, kn_ref, k_refs, page)
    s_chunks = [s + cq - _stack_heads(c) for s, c in zip(s_chunks, c_chunks)]
    s_chunks[-1] = jnp.where(lane <= sub, s_chunks[-1], -jnp.inf)

    mx = s_chunks[0]
    for s in s_chunks[1:]:
        mx = jnp.maximum(mx, s)
    m = jnp.max(mx, axis=1, keepdims=True)
    p_chunks = [jnp.exp(s - m) for s in s_chunks]
    tot = p_chunks[0]
    for p in p_chunks[1:]:
        tot = tot + p
    l = jnp.sum(tot, axis=1, keepdims=True)

    acc = _decode_apply(p_chunks, vn_ref, v_refs, page)
    o_ref[0] = jnp.concatenate(
        [acc[h] / l[h * t:(h + 1) * t, :] for h in range(N_HEAD)], axis=1)


def _sb_decode(tok_refs, page_refs, o_ref, *, page):
    q_ref, kn_ref, vn_ref = tok_refs
    k_refs, v_refs = page_refs
    t = q_ref.shape[1]
    rows = N_HEAD * t

    sub = lax.broadcasted_iota(jnp.int32, (rows, LANES), 0) & (t - 1)
    lane = lax.broadcasted_iota(jnp.int32, (rows, LANES), 1)
    strict_new = lane < sub

    z_chunks = _decode_scores(q_ref, kn_ref, k_refs, page)
    log_beta, log_1m = [], []
    for z in z_chunks:
        lb, l1 = _log_sigmoid_pair(z)
        log_beta.append(lb)
        log_1m.append(l1)
    log_1m[-1] = jnp.where(strict_new, log_1m[-1], 0.0)

    tri2 = _tri_rhs(2, suffix=True)
    later = jnp.zeros((rows, 1), F32)
    a_chunks = [None] * len(z_chunks)
    for c in range(len(z_chunks) - 1, -1, -1):
        a = jnp.exp(log_beta[c] + _suffix_lanes(log_1m[c], tri2) + later)
        if c == len(z_chunks) - 1:
            a = jnp.where(strict_new, a, 0.0)
        a_chunks[c] = a
        later = later + jnp.sum(log_1m[c], axis=1, keepdims=True)

    acc = _decode_apply(a_chunks, vn_ref, v_refs, page)
    o_ref[0] = jnp.concatenate(acc, axis=1)


def _mlp_decode_kernel(pt_ref, hn_ref, wu_ref, wd_ref, init_ref, g_ref, *refs, decode, normalize,
                       n_tok, n_cache):
    tok_refs = refs[:n_tok]
    cache_refs = refs[n_tok:n_tok + n_cache]
    y_ref, o_ref = refs[n_tok + n_cache:n_tok + n_cache + 2]
    bufs = refs[n_tok + n_cache + 2:n_tok + 2 * n_cache + 2]
    sem = refs[n_tok + 2 * n_cache + 2]
    n_pages = bufs[0].shape[1]
    n_steps = pl.num_programs(0) * pl.num_programs(1)
    step = pl.program_id(0) * pl.num_programs(1) + pl.program_id(1)
    slot = lax.rem(step, 2)

    def page_copies(seq, to_slot):
        return [pltpu.make_async_copy(cache.at[pt_ref[seq, p]], buf.at[to_slot, p],
                                      sem.at[to_slot, kind])
                for kind, (cache, buf) in enumerate(zip(cache_refs, bufs))
                for p in range(n_pages)]

    @pl.when(step == 0)
    def _():
        for copy in page_copies(0, 0):
            copy.start()

    following = jnp.minimum(step + 1, n_steps - 1)
    for copy in page_copies(following, 1 - slot):
        copy.start()
    _mlp_step(hn_ref, wu_ref, wd_ref, init_ref, g_ref, y_ref, normalize=normalize)
    for copy in page_copies(step, slot):
        copy.wait()
    decode(tok_refs, [[buf.at[slot, p] for p in range(n_pages)] for buf in bufs], o_ref)

    @pl.when(step == n_steps - 1)
    def _():
        for copy in page_copies(following, 1 - slot):
            copy.wait()


def _mlp_with_decode(name, hn, w_up_bf, w_down_bf, init, g_final, f_lo, n_f, normalize,
                     page_table, q, kn, vn, cache_k, cache_v, lf_new=None, cache_lf=None,
                     tm=512, tf=512):
    m, d = init.shape
    b, t, _ = q.shape
    n_i = m // tm
    assert n_i * n_f == b, "one decode sequence per MLP grid step"
    n_pages = page_table.shape[1]
    page = cache_k.shape[1] // N_HEAD
    seq = lambda i, f: i * n_f + f
    tok = pl.BlockSpec((1, t, W_MIX), lambda i, f, pt: (seq(i, f), 0, 0))
    in_hbm = pl.BlockSpec(memory_space=pl.ANY)

    in_specs = [pl.BlockSpec((tm, d), lambda i, f, pt: (i, 0)),
                pl.BlockSpec((d, tf), lambda i, f, pt: (0, f_lo + f)),
                pl.BlockSpec((tf, d), lambda i, f, pt: (f_lo + f, 0)),
                pl.BlockSpec((tm, d), lambda i, f, pt: (i, 0), pipeline_mode=pl.Buffered(1)),
                pl.BlockSpec((1, d), lambda i, f, pt: (0, 0)),
                tok, tok, tok]
    args = [hn, w_up_bf, w_down_bf, init, g_final.reshape(1, d), q, kn, vn]
    caches = [cache_k, cache_v]
    decode = _sb_decode
    if cache_lf is not None:
        decode = _fox_decode
        in_specs.append(pl.BlockSpec((1, N_HEAD, LANES), lambda i, f, pt: (seq(i, f), 0, 0)))
        args.append(lf_new)
        caches.append(cache_lf)
    n_tok = len(args) - 5
    page_bufs = [pltpu.VMEM((2, n_pages) + c.shape[1:], F32) for c in caches]
    double = (_nbytes((tm, d), BF16) + _nbytes((d, tf), BF16) + _nbytes((tf, d), BF16)
              + _nbytes((tm, d), F32) + (n_tok + 1) * _nbytes((t, W_MIX), F32))
    single = _nbytes((tm, d), F32) + sum(_nbytes(p.shape, F32) for p in page_bufs)
    return pl.pallas_call(
        functools.partial(_mlp_decode_kernel, normalize=normalize, n_tok=n_tok,
                          n_cache=len(caches), decode=functools.partial(decode, page=page)),
        grid_spec=pltpu.PrefetchScalarGridSpec(
            num_scalar_prefetch=1,
            grid=(n_i, n_f),
            in_specs=in_specs + [in_hbm] * len(caches),
            out_specs=[pl.BlockSpec((tm, d), lambda i, f, pt: (i, 0)),
                       pl.BlockSpec((1, t, W_MIX), lambda i, f, pt: (seq(i, f), 0, 0))],
            scratch_shapes=page_bufs + [pltpu.SemaphoreType.DMA((2, len(caches)))],
        ),
        out_shape=[jax.ShapeDtypeStruct((m, d), F32), jax.ShapeDtypeStruct(q.shape, F32)],
        compiler_params=_params(double, single, semantics=("arbitrary", "arbitrary")),
        name=name,
    )(page_table, *args, *caches)


MLP_TILE = 512


def kernel(x_prompt, x_sample, cache_fox_k, cache_fox_v, cache_fox_logf, cache_sb_k, cache_sb_v,
           page_table, g_mix, w_in, b_f, g_out_fox, g_out_sb, w_out, g_ffn, w_up, w_down, g_final):
    depth = w_in.shape[0]
    assert depth == 1, "single-layer trunk"
    b, s, d = x_prompt.shape
    db, t, _ = x_sample.shape
    n_pool, page = cache_fox_k.shape[1], cache_fox_k.shape[2]
    assert page == LANES and cache_fox_k.shape[3:] == (N_HEAD, HEAD_DIM)

    n_main = N_SEG * W_MIX
    w_in_bf = w_in[0, :, :n_main].astype(BF16)
    wg_bf = jnp.pad(w_in[0, :, n_main:], ((0, 0), (0, LANES - N_HEAD))).astype(BF16)
    bg = jnp.pad(b_f[0], (0, LANES - N_HEAD)).reshape(1, LANES).astype(F32)
    w_out_bf = w_out[0].astype(BF16)
    w_up_bf = w_up[0].astype(BF16)
    w_down_bf = w_down[0].astype(BF16)
    mix_out = (g_out_fox[0], g_out_sb[0])

    xp = x_prompt.reshape(b * s, d)
    (qf, kf, vf, qs, ks, vs), lf = _project(_norm_rows(xp, g_mix[0]), w_in_bf, wg_bf, bg)
    xs = x_sample.reshape(db * t, d)
    (qf_s, kf_s, vf_s, qs_s, ks_s, vs_s), lf_s = _project(_norm_rows(xs, g_mix[0]), w_in_bf,
                                                          wg_bf, bg)

    as_seq = lambda a: a.reshape(b, s, a.shape[-1])
    c = _cumsum_seq(as_seq(lf))
    c_t = jnp.swapaxes(c[:, :, :N_HEAD], 1, 2)
    o_f = _fox_prompt(as_seq(qf), as_seq(kf), as_seq(vf), c, c_t)
    o_s = _sb_prompt(as_seq(qs), as_seq(ks), as_seq(vs))
    h, hn = _attn_out(o_f.reshape(b * s, W_MIX), o_s.reshape(b * s, W_MIX), *mix_out, xp,
                      w_out_bf, g_ffn[0])

    as_tok = lambda a: a.reshape(db, t, a.shape[-1])
    rows_view = lambda cache: cache[0].reshape(n_pool, page * N_HEAD, HEAD_DIM)
    lf_cache_t = jnp.swapaxes(cache_fox_logf[0], 1, 2)
    lf_new_t = jnp.swapaxes(as_tok(lf_s)[:, :, :N_HEAD], 1, 2)
    lf_new_t = jnp.pad(lf_new_t, ((0, 0), (0, 0), (0, LANES - t)))
    n_f = db // (b * s // MLP_TILE)
    assert 2 * n_f * MLP_TILE == w_up_bf.shape[1], "two halves cover the hidden width"
    mlp_in = (hn, w_up_bf, w_down_bf)
    y_half, o_f_s = _mlp_with_decode(
        "mlp_fox_decode", *mlp_in, h, g_final, 0, n_f, False, page_table,
        as_tok(qf_s), as_tok(kf_s), as_tok(vf_s), rows_view(cache_fox_k),
        rows_view(cache_fox_v), lf_new=lf_new_t, cache_lf=lf_cache_t,
        tm=MLP_TILE, tf=MLP_TILE)
    y_prompt, o_s_s = _mlp_with_decode(
        "mlp_sb_decode", *mlp_in, y_half, g_final, n_f, n_f, True, page_table,
        as_tok(qs_s), as_tok(ks_s), as_tok(vs_s), rows_view(cache_sb_k),
        rows_view(cache_sb_v), tm=MLP_TILE, tf=MLP_TILE)

    h_s, hn_s = _attn_out(o_f_s.reshape(db * t, W_MIX), o_s_s.reshape(db * t, W_MIX), *mix_out,
                          xs, w_out_bf, g_ffn[0])
    y_sample = _mlp(hn_s, w_up_bf, w_down_bf, h_s, g_final)

    heads = lambda a, n0, n1: a.reshape(1, n0, n1, N_HEAD, HEAD_DIM)
    return (y_prompt.reshape(b, s, d), y_sample.reshape(db, t, d),
            heads(kf, b, s), heads(vf, b, s), lf[:, :N_HEAD].reshape(1, b, s, N_HEAD),
            heads(ks, b, s), heads(vs, b, s),
            heads(kf_s, db, t), heads(vf_s, db, t), lf_s[:, :N_HEAD].reshape(1, db, t, N_HEAD),
            heads(ks_s, db, t), heads(vs_s, db, t))
```

```python
import functools

import jax
import jax.numpy as jnp
from jax import lax
from jax.experimental import pallas as pl
from jax.experimental.pallas import tpu as pltpu

F32 = jnp.float32
BF16 = jnp.bfloat16

EPS = 1e-6
HEAD_DIM = 128
N_HEAD = 8
W_MIX = N_HEAD * HEAD_DIM
SCALE = HEAD_DIM ** -0.5
LOG2E = 1.4426950408889634
LANES = 128
V7X_VMEM_BYTES = 64 * 1024 * 1024
COMPILER_TEMP_BYTES = 12 * 1024 * 1024

NT_DIMS = (((1,), (1,)), ((), ()))


def _vmem_limit(block_bytes, scratch_bytes=0):
    need = 2 * block_bytes + scratch_bytes + COMPILER_TEMP_BYTES
    return int(min(need, V7X_VMEM_BYTES - 4 * 1024 * 1024))


def _nbytes(shape, dtype):
    n = 1
    for s in shape:
        n *= s
    return n * jnp.dtype(dtype).itemsize


def _params(block_bytes, scratch_bytes=0, semantics=None):
    return pltpu.CompilerParams(
        dimension_semantics=semantics,
        vmem_limit_bytes=_vmem_limit(block_bytes, scratch_bytes))


def _rms(x, g):
    ms = jnp.mean(x * x, axis=-1, keepdims=True)
    return x * lax.rsqrt(ms + EPS) * g


def _log_sigmoid_pair(z):
    exp_neg_abs = jnp.exp2(jnp.abs(z) * (-LOG2E))
    log_sig = jnp.minimum(z, 0.0) - jnp.log(1.0 + exp_neg_abs)
    return log_sig, log_sig - z


def _tri_rhs(copies, suffix):
    r = lax.broadcasted_iota(jnp.int32, (copies * LANES, LANES), 0) & (LANES - 1)
    c = lax.broadcasted_iota(jnp.int32, (copies * LANES, LANES), 1)
    cond = (r > c) if suffix else (r <= c)
    return jnp.where(cond, 1.0, 0.0).astype(BF16)


def _split3(x):
    hi = x.astype(BF16)
    r = x - hi.astype(F32)
    mid = r.astype(BF16)
    lo = (r - mid.astype(F32)).astype(BF16)
    return hi, mid, lo


def _prefix_lanes(x, tri3):
    hi, mid, lo = _split3(x)
    return jnp.dot(jnp.concatenate([hi, mid, lo], axis=1), tri3,
                   preferred_element_type=F32)


def _suffix_lanes(x, tri2):
    hi = x.astype(BF16)
    lo = (x - hi.astype(F32)).astype(BF16)
    return jnp.dot(jnp.concatenate([hi, lo], axis=1), tri2,
                   preferred_element_type=F32)


N_SEG = 6


def _proj_kernel(x_ref, g_ref, w_ref, wg_ref, bg_ref, *refs):
    out_refs = refs[:N_SEG + 1]
    xn_ref = refs[N_SEG + 1]
    j = pl.program_id(1)

    @pl.when(j == 0)
    def _():
        xn = _rms(x_ref[...], g_ref[...]).astype(BF16)
        xn_ref[...] = xn
        y = jnp.dot(xn, wg_ref[...], preferred_element_type=F32) + bg_ref[...]
        out_refs[N_SEG][...] = _log_sigmoid_pair(y)[0]

    for seg in range(N_SEG):
        @pl.when(j == seg)
        def _(seg=seg):
            out_refs[seg][...] = jnp.dot(xn_ref[...], w_ref[...], preferred_element_type=F32)


def _project(x, g, w_bf, wg_bf, bg, tm=512):
    m, d = x.shape
    row = lambda i, j: (i, 0)
    const = lambda i, j: (0, 0)
    blk = (_nbytes((tm, d), F32) + _nbytes((d, W_MIX), BF16) + _nbytes((d, LANES), BF16)
           + N_SEG * _nbytes((tm, W_MIX), F32) + _nbytes((tm, LANES), F32))
    outs = pl.pallas_call(
        _proj_kernel,
        grid=(m // tm, N_SEG),
        in_specs=[pl.BlockSpec((tm, d), row),
                  pl.BlockSpec((1, d), const),
                  pl.BlockSpec((d, W_MIX), lambda i, j: (0, j)),
                  pl.BlockSpec((d, LANES), const),
                  pl.BlockSpec((1, LANES), const)],
        out_specs=[pl.BlockSpec((tm, W_MIX), row)] * N_SEG + [pl.BlockSpec((tm, LANES), row)],
        out_shape=[jax.ShapeDtypeStruct((m, W_MIX), F32)] * N_SEG
                  + [jax.ShapeDtypeStruct((m, LANES), F32)],
        scratch_shapes=[pltpu.VMEM((tm, d), BF16)],
        compiler_params=_params(blk, _nbytes((tm, d), BF16),
                                semantics=("arbitrary", "arbitrary")),
        name="project",
    )(x, g.reshape(1, d), w_bf, wg_bf, bg)
    return outs[:N_SEG], outs[N_SEG]


def _attn_out_kernel(of_ref, os_ref, gf_ref, gs_ref, x_ref, w_ref, gn_ref, h_ref, hn_ref):
    nf = _rms(of_ref[...], gf_ref[...]).astype(BF16)
    ns = _rms(os_ref[...], gs_ref[...]).astype(BF16)
    proj = (jnp.dot(nf, w_ref[0:W_MIX, :], preferred_element_type=F32)
            + jnp.dot(ns, w_ref[W_MIX:2 * W_MIX, :], preferred_element_type=F32))
    h = x_ref[...] + proj
    h_ref[...] = h
    hn_ref[...] = _rms(h, gn_ref[...]).astype(BF16)


def _attn_out(o_f, o_s, g_f, g_s, x, w_out_bf, g_ffn, tm=512):
    m, d = x.shape
    blk = (2 * _nbytes((tm, W_MIX), F32) + 2 * _nbytes((tm, d), F32)
           + _nbytes((2 * W_MIX, d), BF16) + _nbytes((tm, d), BF16))
    row = lambda i: (i, 0)
    const = lambda i: (0, 0)
    return pl.pallas_call(
        _attn_out_kernel,
        grid=(m // tm,),
        in_specs=[pl.BlockSpec((tm, W_MIX), row), pl.BlockSpec((tm, W_MIX), row),
                  pl.BlockSpec((1, W_MIX), const), pl.BlockSpec((1, W_MIX), const),
                  pl.BlockSpec((tm, d), row),
                  pl.BlockSpec((2 * W_MIX, d), const),
                  pl.BlockSpec((1, d), const)],
        out_specs=[pl.BlockSpec((tm, d), row), pl.BlockSpec((tm, d), row)],
        out_shape=[jax.ShapeDtypeStruct((m, d), F32), jax.ShapeDtypeStruct((m, d), BF16)],
        compiler_params=_params(blk, semantics=("arbitrary",)),
        name="attn_out",
    )(o_f, o_s, g_f.reshape(1, -1), g_s.reshape(1, -1), x, w_out_bf, g_ffn.reshape(1, -1))


def _mlp_step(hn_ref, wu_ref, wd_ref, init_ref, g_ref, y_ref, *, normalize):
    f = pl.program_id(1)

    @pl.when(f == 0)
    def _():
        y_ref[...] = init_ref[...]

    u = jnp.maximum(jnp.dot(hn_ref[...], wu_ref[...], preferred_element_type=F32), 0.0)
    y_ref[...] += jnp.dot((u * u).astype(BF16), wd_ref[...], preferred_element_type=F32)

    if normalize:
        @pl.when(f == pl.num_programs(1) - 1)
        def _():
            y_ref[...] = _rms(y_ref[...], g_ref[...])


def _mlp_kernel(hn_ref, wu_ref, wd_ref, h_ref, g_ref, y_ref):
    _mlp_step(hn_ref, wu_ref, wd_ref, h_ref, g_ref, y_ref, normalize=True)


def _mlp(hn, w_up_bf, w_down_bf, h, g_final, tm=1024, tf=512):
    m, d = h.shape
    d_ff = w_up_bf.shape[1]
    blk = (_nbytes((tm, d), BF16) + _nbytes((d, tf), BF16) + _nbytes((tf, d), BF16)
           + 2 * _nbytes((tm, d), F32))
    return pl.pallas_call(
        _mlp_kernel,
        grid=(m // tm, d_ff // tf),
        in_specs=[pl.BlockSpec((tm, d), lambda i, f: (i, 0)),
                  pl.BlockSpec((d, tf), lambda i, f: (0, f)),
                  pl.BlockSpec((tf, d), lambda i, f: (f, 0)),
                  pl.BlockSpec((tm, d), lambda i, f: (i, 0)),
                  pl.BlockSpec((1, d), lambda i, f: (0, 0))],
        out_specs=pl.BlockSpec((tm, d), lambda i, f: (i, 0)),
        out_shape=jax.ShapeDtypeStruct((m, d), F32),
        compiler_params=_params(blk, semantics=("arbitrary", "arbitrary")),
        name="mlp",
    )(hn, w_up_bf, w_down_bf, h, g_final.reshape(1, d))


def _cumsum_seq_kernel(x_ref, o_ref):
    s = x_ref.shape[1]
    r = lax.broadcasted_iota(jnp.int32, (LANES, 3 * LANES), 0)
    c = lax.broadcasted_iota(jnp.int32, (LANES, 3 * LANES), 1) & (LANES - 1)
    tri3 = jnp.where(c <= r, 1.0, 0.0).astype(BF16)
    carry = jnp.zeros((1, LANES), F32)
    for ch in range(s // LANES):
        x = x_ref[0, ch * LANES:(ch + 1) * LANES, :]
        hi, mid, lo = _split3(x)
        loc = jnp.dot(tri3, jnp.concatenate([hi, mid, lo], axis=0),
                      preferred_element_type=F32)
        o_ref[0, ch * LANES:(ch + 1) * LANES, :] = loc + carry
        carry = carry + loc[LANES - 1:LANES, :]


def _cumsum_seq(x):
    b, s, _ = x.shape
    blk = 2 * _nbytes((s, LANES), F32)
    return pl.pallas_call(
        _cumsum_seq_kernel,
        grid=(b,),
        in_specs=[pl.BlockSpec((1, s, LANES), lambda i: (i, 0, 0))],
        out_specs=pl.BlockSpec((1, s, LANES), lambda i: (i, 0, 0)),
        out_shape=jax.ShapeDtypeStruct(x.shape, F32),
        compiler_params=_params(blk, semantics=("arbitrary",)),
        name="cumsum_seq",
    )(x)


def _fox_prompt_kernel(q_ref, k_ref, v_ref, c_ref, ct_ref, o_ref, qbf, kbf, vbf, *, tq):
    h = pl.program_id(1)
    s_len = q_ref.shape[1]
    qbf[...] = q_ref[0].astype(BF16)
    kbf[...] = k_ref[0].astype(BF16)
    vbf[...] = v_ref[0].astype(BF16)
    lane = lax.broadcasted_iota(jnp.int32, (tq, LANES), 1)
    qpos = lax.broadcasted_iota(jnp.int32, (tq, tq), 0)
    kpos = lax.broadcasted_iota(jnp.int32, (tq, tq), 1)
    causal = kpos <= qpos
    for i in range(s_len // tq):
        rows = slice(i * tq, (i + 1) * tq)
        q = qbf[rows, :]
        cq = jnp.sum(jnp.where(lane == h, c_ref[0, rows, :], 0.0), axis=1, keepdims=True)
        cq = cq * LOG2E
        m = jnp.full((tq, 1), -jnp.inf, F32)
        l = jnp.zeros((tq, 1), F32)
        acc = jnp.zeros((tq, HEAD_DIM), F32)
        for kb in [i] + list(range(i)):
            cols = slice(kb * tq, (kb + 1) * tq)
            s = lax.dot_general(q, kbf[cols, :], NT_DIMS, preferred_element_type=F32)
            s = s * (SCALE * LOG2E) + cq - ct_ref[0, pl.ds(h, 1), cols] * LOG2E
            if kb == i:
                s = jnp.where(causal, s, -jnp.inf)
            m_new = jnp.maximum(m, jnp.max(s, axis=1, keepdims=True))
            alpha = jnp.exp2(m - m_new)
            p = jnp.exp2(s - m_new)
            l = alpha * l + jnp.sum(p, axis=1, keepdims=True)
            acc = alpha * acc + jnp.dot(p.astype(BF16), vbf[cols, :],
                                        preferred_element_type=F32)
            m = m_new
        o_ref[0, rows, :] = acc / l


def _sb_prompt_kernel(q_ref, k_ref, v_ref, o_ref, qbf, kbf, vbf, *, tq):
    s_len = q_ref.shape[1]
    qbf[...] = q_ref[0].astype(BF16)
    kbf[...] = k_ref[0].astype(BF16)
    vbf[...] = v_ref[0].astype(BF16)
    tri2 = _tri_rhs(2, suffix=True)
    qpos = lax.broadcasted_iota(jnp.int32, (tq, tq), 0)
    kpos = lax.broadcasted_iota(jnp.int32, (tq, tq), 1)
    strict = kpos < qpos
    for i in range(s_len // tq):
        rows = slice(i * tq, (i + 1) * tq)
        q = qbf[rows, :]
        later = jnp.zeros((tq, 1), F32)
        acc = jnp.zeros((tq, HEAD_DIM), F32)
        for kb in range(i, -1, -1):
            cols = slice(kb * tq, (kb + 1) * tq)
            z = lax.dot_general(q, kbf[cols, :], NT_DIMS, preferred_element_type=F32) * SCALE
            log_beta, log_1m = _log_sigmoid_pair(z)
            if kb == i:
                log_1m = jnp.where(strict, log_1m, 0.0)
            parts = [None] * (tq // LANES)
            for sub in range(tq // LANES - 1, -1, -1):
                ln = slice(sub * LANES, (sub + 1) * LANES)
                l1 = log_1m[:, ln]
                a = jnp.exp(log_beta[:, ln] + _suffix_lanes(l1, tri2) + later)
                if kb == i:
                    a = jnp.where(strict[:, ln], a, 0.0)
                parts[sub] = a.astype(BF16)
                later = later + jnp.sum(l1, axis=1, keepdims=True)
            acc = acc + jnp.dot(jnp.concatenate(parts, axis=1), vbf[cols, :],
                                preferred_element_type=F32)
        o_ref[0, rows, :] = acc


def _prompt_attention(kernel, q, k, v, extra, extra_specs, tq, name):
    b, s, _ = q.shape
    head = lambda bi, h: (bi, 0, h)
    blk = 4 * _nbytes((s, HEAD_DIM), F32) + _nbytes((s, LANES), F32) + _nbytes((N_HEAD, s), F32)
    scratch = 3 * _nbytes((s, HEAD_DIM), BF16)
    seq = pl.BlockSpec((1, s, HEAD_DIM), head)
    return pl.pallas_call(
        functools.partial(kernel, tq=tq),
        grid=(b, N_HEAD),
        in_specs=[seq, seq, seq] + extra_specs,
        out_specs=seq,
        out_shape=jax.ShapeDtypeStruct(q.shape, F32),
        scratch_shapes=[pltpu.VMEM((s, HEAD_DIM), BF16)] * 3,
        compiler_params=_params(blk, scratch, semantics=("arbitrary", "arbitrary")),
        name=name,
    )(q, k, v, *extra)


def _fox_prompt(q, k, v, c, c_t, tq=256):
    s = q.shape[1]
    specs = [pl.BlockSpec((1, s, LANES), lambda bi, h: (bi, 0, 0)),
             pl.BlockSpec((1, N_HEAD, s), lambda bi, h: (bi, 0, 0))]
    return _prompt_attention(_fox_prompt_kernel, q, k, v, (c, c_t), specs, tq, "fox_prompt")


def _sb_prompt(q, k, v, tq=256):
    return _prompt_attention(_sb_prompt_kernel, q, k, v, (), [], tq, "sb_prompt")


def _head_rows(page_ref, h, n):
    return page_ref[pl.ds(h, n, stride=N_HEAD), :]


def _stack_heads(rows8):
    t = 8
    return jnp.concatenate(
        [jnp.broadcast_to(rows8[h:h + 1, :], (t, LANES)) for h in range(N_HEAD)], axis=0)


def _decode_scores(q_ref, kn_ref, k_refs, page):
    t = q_ref.shape[1]
    q = q_ref[0]
    first = lax.broadcasted_iota(jnp.int32, (t, 2 * HEAD_DIM), 1) < HEAD_DIM
    qpair = []
    for pr in range(N_HEAD // 2):
        qq = q[:, pr * 2 * HEAD_DIM:(pr + 1) * 2 * HEAD_DIM]
        qpair.append(jnp.concatenate([jnp.where(first, qq, 0.0), jnp.where(first, 0.0, qq)],
                                     axis=0).astype(BF16))

    def scores(pair_keys):
        outs = [lax.dot_general(qpair[pr], pair_keys(pr), NT_DIMS, preferred_element_type=F32)
                for pr in range(N_HEAD // 2)]
        return jnp.concatenate(outs, axis=0) * SCALE

    chunks = []
    for kr in k_refs:
        chunks.append(scores(lambda pr: jnp.concatenate(
            [_head_rows(kr, 2 * pr, page), _head_rows(kr, 2 * pr + 1, page)],
            axis=1).astype(BF16)))
    kn = jnp.concatenate([kn_ref[0], jnp.zeros((page - t, W_MIX), F32)], axis=0).astype(BF16)
    chunks.append(scores(lambda pr: kn[:, pr * 2 * HEAD_DIM:(pr + 1) * 2 * HEAD_DIM]))
    return chunks


def _decode_apply(w_chunks, vn_ref, v_refs, page):
    t = vn_ref.shape[1]
    acc = [jnp.zeros((t, HEAD_DIM), F32) for _ in range(N_HEAD)]

    def apply(w, pair_values):
        for pr in range(N_HEAD // 2):
            o = jnp.dot(w[pr * 2 * t:(pr + 1) * 2 * t, :].astype(BF16), pair_values(pr),
                        preferred_element_type=F32)
            acc[2 * pr] = acc[2 * pr] + o[:t, :HEAD_DIM]
            acc[2 * pr + 1] = acc[2 * pr + 1] + o[t:, HEAD_DIM:]

    for w, vr in zip(w_chunks[:-1], v_refs):
        apply(w, lambda pr: jnp.concatenate(
            [_head_rows(vr, 2 * pr, page), _head_rows(vr, 2 * pr + 1, page)],
            axis=1).astype(BF16))
    vn = jnp.concatenate([vn_ref[0], jnp.zeros((page - t, W_MIX), F32)], axis=0).astype(BF16)
    apply(w_chunks[-1], lambda pr: vn[:, pr * 2 * HEAD_DIM:(pr + 1) * 2 * HEAD_DIM])
    return acc


def _fox_decode(tok_refs, page_refs, o_ref, *, page):
    q_ref, kn_ref, vn_ref, lfn_ref = tok_refs
    k_refs, v_refs, lf_refs = page_refs
    t = q_ref.shape[1]
    rows = N_HEAD * t

    tri3 = _tri_rhs(3, suffix=False)
    carry = jnp.zeros((N_HEAD, 1), F32)
    c_chunks = []
    for lr in lf_refs:
        loc = _prefix_lanes(lr[...], tri3)
        c_chunks.append(loc + carry)
        carry = carry + loc[:, LANES - 1:LANES]
    c_new = _prefix_lanes(lfn_ref[0], tri3) + carry
    c_chunks.append(c_new)

    sub = lax.broadcasted_iota(jnp.int32, (rows, LANES), 0) & (t - 1)
    lane = lax.broadcasted_iota(jnp.int32, (rows, LANES), 1)
    c_new_rows = _stack_heads(c_new)
    cq = jnp.sum(jnp.where(lane == sub, c_new_rows, 0.0), axis=1, keepdims=True)

    s_chunks = _decode_scores(q_ref, kn_ref, k_refs, page)
    s_chunks = [s + cq - _stack_heads(c) for s, c in zip(s_chunks, c_chunks)]
    s_chunks[-1] = jnp.where(lane <= sub, s_chunks[-1], -jnp.inf)

    mx = s_chunks[0]
    for s in s_chunks[1:]:
        mx = jnp.maximum(mx, s)
    m = jnp.max(mx, axis=1, keepdims=True)
    p_chunks = [jnp.exp(s - m) for s in s_chunks]
    tot = p_chunks[0]
    for p in p_chunks[1:]:
        tot = tot + p
    l = jnp.sum(tot, axis=1, keepdims=True)

    acc = _decode_apply(p_chunks, vn_ref, v_refs, page)
    o_ref[0] = jnp.concatenate(
        [acc[h] / l[h * t:(h + 1) * t, :] for h in range(N_HEAD)], axis=1)


def _sb_decode(tok_refs, page_refs, o_ref, *, page):
    q_ref, kn_ref, vn_ref = tok_refs
    k_refs, v_refs = page_refs
    t = q_ref.shape[1]
    rows = N_HEAD * t

    sub = lax.broadcasted_iota(jnp.int32, (rows, LANES), 0) & (t - 1)
    lane = lax.broadcasted_iota(jnp.int32, (rows, LANES), 1)
    strict_new = lane < sub

    z_chunks = _decode_scores(q_ref, kn_ref, k_refs, page)
    log_beta, log_1m = [], []
    for z in z_chunks:
        lb, l1 = _log_sigmoid_pair(z)
        log_beta.append(lb)
        log_1m.append(l1)
    log_1m[-1] = jnp.where(strict_new, log_1m[-1], 0.0)

    tri2 = _tri_rhs(2, suffix=True)
    later = jnp.zeros((rows, 1), F32)
    a_chunks = [None] * len(z_chunks)
    for c in range(len(z_chunks) - 1, -1, -1):
        a = jnp.exp(log_beta[c] + _suffix_lanes(log_1m[c], tri2) + later)
        if c == len(z_chunks) - 1:
            a = jnp.where(strict_new, a, 0.0)
        a_chunks[c] = a
        later = later + jnp.sum(log_1m[c], axis=1, keepdims=True)

    acc = _decode_apply(a_chunks, vn_ref, v_refs, page)
    o_ref[0] = jnp.concatenate(acc, axis=1)


def _mlp_decode_kernel(pt_ref, hn_ref, wu_ref, wd_ref, init_ref, g_ref, *refs, decode, normalize,
                       n_tok, n_cache):
    tok_refs = refs[:n_tok]
    cache_refs = refs[n_tok:n_tok + n_cache]
    y_ref, o_ref = refs[n_tok + n_cache:n_tok + n_cache + 2]
    bufs = refs[n_tok + n_cache + 2:n_tok + 2 * n_cache + 2]
    sem = refs[n_tok + 2 * n_cache + 2]
    n_pages = bufs[0].shape[1]
    n_steps = pl.num_programs(0) * pl.num_programs(1)
    step = pl.program_id(0) * pl.num_programs(1) + pl.program_id(1)
    slot = lax.rem(step, 2)

    def page_copies(seq, to_slot):
        return [pltpu.make_async_copy(cache.at[pt_ref[seq, p]], buf.at[to_slot, p],
                                      sem.at[to_slot, kind])
                for kind, (cache, buf) in enumerate(zip(cache_refs, bufs))
                for p in range(n_pages)]

    @pl.when(step == 0)
    def _():
        for copy in page_copies(0, 0):
            copy.start()

    following = jnp.minimum(step + 1, n_steps - 1)
    for copy in page_copies(following, 1 - slot):
        copy.start()
    _mlp_step(hn_ref, wu_ref, wd_ref, init_ref, g_ref, y_ref, normalize=normalize)
    for copy in page_copies(step, slot):
        copy.wait()
    decode(tok_refs, [[buf.at[slot, p] for p in range(n_pages)] for buf in bufs], o_ref)

    @pl.when(step == n_steps - 1)
    def _():
        for copy in page_copies(following, 1 - slot):
            copy.wait()


def _mlp_with_decode(name, hn, w_up_bf, w_down_bf, init, g_final, f_lo, n_f, normalize,
                     page_table, q, kn, vn, cache_k, cache_v, lf_new=None, cache_lf=None,
                     tm=512, tf=512):
    m, d = init.shape
    b, t, _ = q.shape
    n_i = m // tm
    assert n_i * n_f == b, "one decode sequence per MLP grid step"
    n_pages = page_table.shape[1]
    page = cache_k.shape[1] // N_HEAD
    seq = lambda i, f: i * n_f + f
    tok = pl.BlockSpec((1, t, W_MIX), lambda i, f, pt: (seq(i, f), 0, 0))
    in_hbm = pl.BlockSpec(memory_space=pl.ANY)

    in_specs = [pl.BlockSpec((tm, d), lambda i, f, pt: (i, 0)),
                pl.BlockSpec((d, tf), lambda i, f, pt: (0, f_lo + f)),
                pl.BlockSpec((tf, d), lambda i, f, pt: (f_lo + f, 0)),
                pl.BlockSpec((tm, d), lambda i, f, pt: (i, 0), pipeline_mode=pl.Buffered(1)),
                pl.BlockSpec((1, d), lambda i, f, pt: (0, 0)),
                tok, tok, tok]
    args = [hn, w_up_bf, w_down_bf, init, g_final.reshape(1, d), q, kn, vn]
    caches = [cache_k, cache_v]
    decode = _sb_decode
    if cache_lf is not None:
        decode = _fox_decode
        in_specs.append(pl.BlockSpec((1, N_HEAD, LANES), lambda i, f, pt: (seq(i, f), 0, 0)))
        args.append(lf_new)
        caches.append(cache_lf)
    n_tok = len(args) - 5
    page_bufs = [pltpu.VMEM((2, n_pages) + c.shape[1:], F32) for c in caches]
    double = (_nbytes((tm, d), BF16) + _nbytes((d, tf), BF16) + _nbytes((tf, d), BF16)
              + _nbytes((tm, d), F32) + (n_tok + 1) * _nbytes((t, W_MIX), F32))
    single = _nbytes((tm, d), F32) + sum(_nbytes(p.shape, F32) for p in page_bufs)
    return pl.pallas_call(
        functools.partial(_mlp_decode_kernel, normalize=normalize, n_tok=n_tok,
                          n_cache=len(caches), decode=functools.partial(decode, page=page)),
        grid_spec=pltpu.PrefetchScalarGridSpec(
            num_scalar_prefetch=1,
            grid=(n_i, n_f),
            in_specs=in_specs + [in_hbm] * len(caches),
            out_specs=[pl.BlockSpec((tm, d), lambda i, f, pt: (i, 0)),
                       pl.BlockSpec((1, t, W_MIX), lambda i, f, pt: (seq(i, f), 0, 0))],
            scratch_shapes=page_bufs + [pltpu.SemaphoreType.DMA((2, len(caches)))],
        ),
        out_shape=[jax.ShapeDtypeStruct((m, d), F32), jax.ShapeDtypeStruct(q.shape, F32)],
        compiler_params=_params(double, single, semantics=("arbitrary", "arbitrary")),
        name=name,
    )(page_table, *args, *caches)


MLP_TILE = 512


def kernel(x_prompt, x_sample, cache_fox_k, cache_fox_v, cache_fox_logf, cache_sb_k, cache_sb_v,
           page_table, g_mix, w_in, b_f, g_out_fox, g_out_sb, w_out, g_ffn, w_up, w_down, g_final):
    depth = w_in.shape[0]
    assert depth == 1, "single-layer trunk"
    b, s, d = x_prompt.shape
    db, t, _ = x_sample.shape
    n_pool, page = cache_fox_k.shape[1], cache_fox_k.shape[2]
    assert page == LANES and cache_fox_k.shape[3:] == (N_HEAD, HEAD_DIM)

    n_main = N_SEG * W_MIX
    w_in_bf = w_in[0, :, :n_main].astype(BF16)
    wg_bf = jnp.pad(w_in[0, :, n_main:], ((0, 0), (0, LANES - N_HEAD))).astype(BF16)
    bg = jnp.pad(b_f[0], (0, LANES - N_HEAD)).reshape(1, LANES).astype(F32)
    w_out_bf = w_out[0].astype(BF16)
    w_up_bf = w_up[0].astype(BF16)
    w_down_bf = w_down[0].astype(BF16)
    mix_out = (g_out_fox[0], g_out_sb[0])

    xp = x_prompt.reshape(b * s, d)
    (qf, kf, vf, qs, ks, vs), lf = _project(xp, g_mix[0], w_in_bf, wg_bf, bg)
    xs = x_sample.reshape(db * t, d)
    (qf_s, kf_s, vf_s, qs_s, ks_s, vs_s), lf_s = _project(xs, g_mix[0], w_in_bf, wg_bf, bg)

    as_seq = lambda a: a.reshape(b, s, a.shape[-1])
    c = _cumsum_seq(as_seq(lf))
    c_t = jnp.swapaxes(c[:, :, :N_HEAD], 1, 2)
    o_f = _fox_prompt(as_seq(qf), as_seq(kf), as_seq(vf), c, c_t)
    o_s = _sb_prompt(as_seq(qs), as_seq(ks), as_seq(vs))
    h, hn = _attn_out(o_f.reshape(b * s, W_MIX), o_s.reshape(b * s, W_MIX), *mix_out, xp,
                      w_out_bf, g_ffn[0])

    as_tok = lambda a: a.reshape(db, t, a.shape[-1])
    rows_view = lambda cache: cache[0].reshape(n_pool, page * N_HEAD, HEAD_DIM)
    lf_cache_t = jnp.swapaxes(cache_fox_logf[0], 1, 2)
    lf_new_t = jnp.swapaxes(as_tok(lf_s)[:, :, :N_HEAD], 1, 2)
    lf_new_t = jnp.pad(lf_new_t, ((0, 0), (0, 0), (0, LANES - t)))
    n_f = db // (b * s // MLP_TILE)
    assert 2 * n_f * MLP_TILE == w_up_bf.shape[1], "two halves cover the hidden width"
    mlp_in = (hn, w_up_bf, w_down_bf)
    y_half, o_f_s = _mlp_with_decode(
        "mlp_fox_decode", *mlp_in, h, g_final, 0, n_f, False, page_table,
        as_tok(qf_s), as_tok(kf_s), as_tok(vf_s), rows_view(cache_fox_k),
        rows_view(cache_fox_v), lf_new=lf_new_t, cache_lf=lf_cache_t,
        tm=MLP_TILE, tf=MLP_TILE)
    y_prompt, o_s_s = _mlp_with_decode(
        "mlp_sb_decode", *mlp_in, y_half, g_final, n_f, n_f, True, page_table,
        as_tok(qs_s), as_tok(ks_s), as_tok(vs_s), rows_view(cache_sb_k),
        rows_view(cache_sb_v), tm=MLP_TILE, tf=MLP_TILE)

    h_s, hn_s = _attn_out(o_f_s.reshape(db * t, W_MIX), o_s_s.reshape(db * t, W_MIX), *mix_out,
                          xs, w_out_bf, g_ffn[0])
    y_sample = _mlp(hn_s, w_up_bf, w_down_bf, h_s, g_final)

    heads = lambda a, n0, n1: a.reshape(1, n0, n1, N_HEAD, HEAD_DIM)
    return (y_prompt.reshape(b, s, d), y_sample.reshape(db, t, d),
            heads(kf, b, s), heads(vf, b, s), lf[:, :N_HEAD].reshape(1, b, s, N_HEAD),
            heads(ks, b, s), heads(vs, b, s),
            heads(kf_s, db, t), heads(vf_s, db, t), lf_s[:, :N_HEAD].reshape(1, db, t, N_HEAD),
            heads(ks_s, db, t), heads(vs_s, db, t))
```

```python
import functools

import jax
import jax.numpy as jnp
from jax import lax
from jax.experimental import pallas as pl
from jax.experimental.pallas import tpu as pltpu

F32 = jnp.float32
BF16 = jnp.bfloat16

EPS = 1e-6
HEAD_DIM = 128
N_HEAD = 8
W_MIX = N_HEAD * HEAD_DIM
SCALE = HEAD_DIM ** -0.5
LOG2E = 1.4426950408889634
LANES = 128
V7X_VMEM_BYTES = 64 * 1024 * 1024
COMPILER_TEMP_BYTES = 12 * 1024 * 1024

NT_DIMS = (((1,), (1,)), ((), ()))


def _vmem_limit(block_bytes, scratch_bytes=0):
    need = 2 * block_bytes + scratch_bytes + COMPILER_TEMP_BYTES
    return int(min(need, V7X_VMEM_BYTES - 4 * 1024 * 1024))


def _nbytes(shape, dtype):
    n = 1
    for s in shape:
        n *= s
    return n * jnp.dtype(dtype).itemsize


def _params(block_bytes, scratch_bytes=0, semantics=None):
    return pltpu.CompilerParams(
        dimension_semantics=semantics,
        vmem_limit_bytes=_vmem_limit(block_bytes, scratch_bytes))


def _rms(x, g):
    ms = jnp.mean(x * x, axis=-1, keepdims=True)
    return x * lax.rsqrt(ms + EPS) * g


def _log_sigmoid_pair(z):
    exp_neg_abs = jnp.exp2(jnp.abs(z) * (-LOG2E))
    log_sig = jnp.minimum(z, 0.0) - jnp.log(1.0 + exp_neg_abs)
    return log_sig, log_sig - z


def _tri_rhs(copies, suffix):
    r = lax.broadcasted_iota(jnp.int32, (copies * LANES, LANES), 0) & (LANES - 1)
    c = lax.broadcasted_iota(jnp.int32, (copies * LANES, LANES), 1)
    cond = (r > c) if suffix else (r <= c)
    return jnp.where(cond, 1.0, 0.0).astype(BF16)


def _split3(x):
    hi = x.astype(BF16)
    r = x - hi.astype(F32)
    mid = r.astype(BF16)
    lo = (r - mid.astype(F32)).astype(BF16)
    return hi, mid, lo


def _prefix_lanes(x, tri3):
    hi, mid, lo = _split3(x)
    return jnp.dot(jnp.concatenate([hi, mid, lo], axis=1), tri3,
                   preferred_element_type=F32)


def _suffix_lanes(x, tri2):
    hi = x.astype(BF16)
    lo = (x - hi.astype(F32)).astype(BF16)
    return jnp.dot(jnp.concatenate([hi, lo], axis=1), tri2,
                   preferred_element_type=F32)


N_SEG = 6


def _proj_kernel(x_ref, g_ref, w_ref, wg_ref, bg_ref, *refs):
    out_refs = refs[:N_SEG + 1]
    xn_ref = refs[N_SEG + 1]
    j = pl.program_id(1)

    @pl.when(j == 0)
    def _():
        xn = _rms(x_ref[...], g_ref[...]).astype(BF16)
        xn_ref[...] = xn
        y = jnp.dot(xn, wg_ref[...], preferred_element_type=F32) + bg_ref[...]
        out_refs[N_SEG][...] = _log_sigmoid_pair(y)[0]

    for seg in range(N_SEG):
        @pl.when(j == seg)
        def _(seg=seg):
            out_refs[seg][...] = jnp.dot(xn_ref[...], w_ref[...], preferred_element_type=F32)


def _project(x, g, w_bf, wg_bf, bg, tm=512):
    m, d = x.shape
    row = lambda i, j: (i, 0)
    const = lambda i, j: (0, 0)
    blk = (_nbytes((tm, d), F32) + _nbytes((d, W_MIX), BF16) + _nbytes((d, LANES), BF16)
           + N_SEG * _nbytes((tm, W_MIX), F32) + _nbytes((tm, LANES), F32))
    outs = pl.pallas_call(
        _proj_kernel,
        grid=(m // tm, N_SEG),
        in_specs=[pl.BlockSpec((tm, d), row),
                  pl.BlockSpec((1, d), const),
                  pl.BlockSpec((d, W_MIX), lambda i, j: (0, j)),
                  pl.BlockSpec((d, LANES), const),
                  pl.BlockSpec((1, LANES), const)],
        out_specs=[pl.BlockSpec((tm, W_MIX), row)] * N_SEG + [pl.BlockSpec((tm, LANES), row)],
        out_shape=[jax.ShapeDtypeStruct((m, W_MIX), F32)] * N_SEG
                  + [jax.ShapeDtypeStruct((m, LANES), F32)],
        scratch_shapes=[pltpu.VMEM((tm, d), BF16)],
        compiler_params=_params(blk, _nbytes((tm, d), BF16),
                                semantics=("arbitrary", "arbitrary")),
        name="project",
    )(x, g.reshape(1, d), w_bf, wg_bf, bg)
    return outs[:N_SEG], outs[N_SEG]


def _attn_out_kernel(of_ref, os_ref, gf_ref, gs_ref, x_ref, w_ref, gn_ref, h_ref, hn_ref):
    nf = _rms(of_ref[...], gf_ref[...]).astype(BF16)
    ns = _rms(os_ref[...], gs_ref[...]).astype(BF16)
    proj = (jnp.dot(nf, w_ref[0:W_MIX, :], preferred_element_type=F32)
            + jnp.dot(ns, w_ref[W_MIX:2 * W_MIX, :], preferred_element_type=F32))
    h = x_ref[...] + proj
    h_ref[...] = h
    hn_ref[...] = _rms(h, gn_ref[...]).astype(BF16)


def _attn_out(o_f, o_s, g_f, g_s, x, w_out_bf, g_ffn, tm=512):
    m, d = x.shape
    blk = (2 * _nbytes((tm, W_MIX), F32) + 2 * _nbytes((tm, d), F32)
           + _nbytes((2 * W_MIX, d), BF16) + _nbytes((tm, d), BF16))
    row = lambda i: (i, 0)
    const = lambda i: (0, 0)
    return pl.pallas_call(
        _attn_out_kernel,
        grid=(m // tm,),
        in_specs=[pl.BlockSpec((tm, W_MIX), row), pl.BlockSpec((tm, W_MIX), row),
                  pl.BlockSpec((1, W_MIX), const), pl.BlockSpec((1, W_MIX), const),
                  pl.BlockSpec((tm, d), row),
                  pl.BlockSpec((2 * W_MIX, d), const),
                  pl.BlockSpec((1, d), const)],
        out_specs=[pl.BlockSpec((tm, d), row), pl.BlockSpec((tm, d), row)],
        out_shape=[jax.ShapeDtypeStruct((m, d), F32), jax.ShapeDtypeStruct((m, d), BF16)],
        compiler_params=_params(blk, semantics=("arbitrary",)),
        name="attn_out",
    )(o_f, o_s, g_f.reshape(1, -1), g_s.reshape(1, -1), x, w_out_bf, g_ffn.reshape(1, -1))


def _mlp_step(hn_ref, wu_ref, wd_ref, init_ref, g_ref, y_ref, *, normalize):
    f = pl.program_id(1)

    @pl.when(f == 0)
    def _():
        y_ref[...] = init_ref[...]

    u = jnp.maximum(jnp.dot(hn_ref[...], wu_ref[...], preferred_element_type=F32), 0.0)
    y_ref[...] += jnp.dot((u * u).astype(BF16), wd_ref[...], preferred_element_type=F32)

    if normalize:
        @pl.when(f == pl.num_programs(1) - 1)
        def _():
            y_ref[...] = _rms(y_ref[...], g_ref[...])


def _mlp_kernel(hn_ref, wu_ref, wd_ref, h_ref, g_ref, y_ref):
    _mlp_step(hn_ref, wu_ref, wd_ref, h_ref, g_ref, y_ref, normalize=True)


def _mlp(hn, w_up_bf, w_down_bf, h, g_final, tm=1024, tf=512):
    m, d = h.shape
    d_ff = w_up_bf.shape[1]
    blk = (_nbytes((tm, d), BF16) + _nbytes((d, tf), BF16) + _nbytes((tf, d), BF16)
           + 2 * _nbytes((tm, d), F32))
    return pl.pallas_call(
        _mlp_kernel,
        grid=(m // tm, d_ff // tf),
        in_specs=[pl.BlockSpec((tm, d), lambda i, f: (i, 0)),
                  pl.BlockSpec((d, tf), lambda i, f: (0, f)),
                  pl.BlockSpec((tf, d), lambda i, f: (f, 0)),
                  pl.BlockSpec((tm, d), lambda i, f: (i, 0)),
                  pl.BlockSpec((1, d), lambda i, f: (0, 0))],
        out_specs=pl.BlockSpec((tm, d), lambda i, f: (i, 0)),
        out_shape=jax.ShapeDtypeStruct((m, d), F32),
        compiler_params=_params(blk, semantics=("arbitrary", "arbitrary")),
        name="mlp",
    )(hn, w_up_bf, w_down_bf, h, g_final.reshape(1, d))


def _cumsum_seq_kernel(x_ref, o_ref):
    s = x_ref.shape[1]
    r = lax.broadcasted_iota(jnp.int32, (LANES, 3 * LANES), 0)
    c = lax.broadcasted_iota(jnp.int32, (LANES, 3 * LANES), 1) & (LANES - 1)
    tri3 = jnp.where(c <= r, 1.0, 0.0).astype(BF16)
    carry = jnp.zeros((1, LANES), F32)
    for ch in range(s // LANES):
        x = x_ref[0, ch * LANES:(ch + 1) * LANES, :]
        hi, mid, lo = _split3(x)
        loc = jnp.dot(tri3, jnp.concatenate([hi, mid, lo], axis=0),
                      preferred_element_type=F32)
        o_ref[0, ch * LANES:(ch + 1) * LANES, :] = loc + carry
        carry = carry + loc[LANES - 1:LANES, :]


def _cumsum_seq(x):
    b, s, _ = x.shape
    blk = 2 * _nbytes((s, LANES), F32)
    return pl.pallas_call(
        _cumsum_seq_kernel,
        grid=(b,),
        in_specs=[pl.BlockSpec((1, s, LANES), lambda i: (i, 0, 0))],
        out_specs=pl.BlockSpec((1, s, LANES), lambda i: (i, 0, 0)),
        out_shape=jax.ShapeDtypeStruct(x.shape, F32),
        compiler_params=_params(blk, semantics=("arbitrary",)),
        name="cumsum_seq",
    )(x)


def _fox_prompt_kernel(q_ref, k_ref, v_ref, c_ref, ct_ref, o_ref, qbf, kbf, vbf, *, tq):
    h = pl.program_id(1)
    s_len = q_ref.shape[1]
    qbf[...] = q_ref[0].astype(BF16)
    kbf[...] = k_ref[0].astype(BF16)
    vbf[...] = v_ref[0].astype(BF16)
    lane = lax.broadcasted_iota(jnp.int32, (tq, LANES), 1)
    qpos = lax.broadcasted_iota(jnp.int32, (tq, tq), 0)
    kpos = lax.broadcasted_iota(jnp.int32, (tq, tq), 1)
    causal = kpos <= qpos
    for i in range(s_len // tq):
        rows = slice(i * tq, (i + 1) * tq)
        q = qbf[rows, :]
        cq = jnp.sum(jnp.where(lane == h, c_ref[0, rows, :], 0.0), axis=1, keepdims=True)
        cq = cq * LOG2E
        m = jnp.full((tq, 1), -jnp.inf, F32)
        l = jnp.zeros((tq, 1), F32)
        acc = jnp.zeros((tq, HEAD_DIM), F32)
        for kb in [i] + list(range(i)):
            cols = slice(kb * tq, (kb + 1) * tq)
            s = lax.dot_general(q, kbf[cols, :], NT_DIMS, preferred_element_type=F32)
            s = s * (SCALE * LOG2E) + cq - ct_ref[0, pl.ds(h, 1), cols] * LOG2E
            if kb == i:
                s = jnp.where(causal, s, -jnp.inf)
            m_new = jnp.maximum(m, jnp.max(s, axis=1, keepdims=True))
            alpha = jnp.exp2(m - m_new)
            p = jnp.exp2(s - m_new)
            l = alpha * l + jnp.sum(p, axis=1, keepdims=True)
            acc = alpha * acc + jnp.dot(p.astype(BF16), vbf[cols, :],
                                        preferred_element_type=F32)
            m = m_new
        o_ref[0, rows, :] = acc / l


def _sb_prompt_kernel(q_ref, k_ref, v_ref, *refs, tq):
    n_cast = (len(refs) - 4) // 2
    o_ref = refs[n_cast]
    qbf, kbf, vbf = refs[2 * n_cast + 1:]
    for src, dst in zip(refs[:n_cast], refs[n_cast + 1:2 * n_cast + 1]):
        dst[...] = src[...].astype(BF16)
    s_len = q_ref.shape[1]
    qbf[...] = q_ref[0].astype(BF16)
    kbf[...] = k_ref[0].astype(BF16)
    vbf[...] = v_ref[0].astype(BF16)
    tri2 = _tri_rhs(2, suffix=True)
    qpos = lax.broadcasted_iota(jnp.int32, (tq, tq), 0)
    kpos = lax.broadcasted_iota(jnp.int32, (tq, tq), 1)
    strict = kpos < qpos
    for i in range(s_len // tq):
        rows = slice(i * tq, (i + 1) * tq)
        q = qbf[rows, :]
        later = jnp.zeros((tq, 1), F32)
        acc = jnp.zeros((tq, HEAD_DIM), F32)
        for kb in range(i, -1, -1):
            cols = slice(kb * tq, (kb + 1) * tq)
            z = lax.dot_general(q, kbf[cols, :], NT_DIMS, preferred_element_type=F32) * SCALE
            log_beta, log_1m = _log_sigmoid_pair(z)
            if kb == i:
                log_1m = jnp.where(strict, log_1m, 0.0)
            parts = [None] * (tq // LANES)
            for sub in range(tq // LANES - 1, -1, -1):
                ln = slice(sub * LANES, (sub + 1) * LANES)
                l1 = log_1m[:, ln]
                a = jnp.exp(log_beta[:, ln] + _suffix_lanes(l1, tri2) + later)
                if kb == i:
                    a = jnp.where(strict[:, ln], a, 0.0)
                parts[sub] = a.astype(BF16)
                later = later + jnp.sum(l1, axis=1, keepdims=True)
            acc = acc + jnp.dot(jnp.concatenate(parts, axis=1), vbf[cols, :],
                                preferred_element_type=F32)
        o_ref[0, rows, :] = acc


def _prompt_attention(kernel, q, k, v, extra, extra_specs, tq, name, to_bf16=()):
    b, s, _ = q.shape
    head = lambda bi, h: (bi, 0, h)
    blk = 4 * _nbytes((s, HEAD_DIM), F32) + _nbytes((s, LANES), F32) + _nbytes((N_HEAD, s), F32)
    scratch = 3 * _nbytes((s, HEAD_DIM), BF16)
    seq = pl.BlockSpec((1, s, HEAD_DIM), head)
    slabs = []
    for w in to_bf16:
        rows = w.shape[0] // (b * N_HEAD)
        assert rows * b * N_HEAD == w.shape[0], "one slab of rows per grid step"
        slabs.append(pl.BlockSpec((rows, w.shape[1]), lambda bi, h: (bi * N_HEAD + h, 0)))
        blk += _nbytes((rows, w.shape[1]), F32) + _nbytes((rows, w.shape[1]), BF16)
    return pl.pallas_call(
        functools.partial(kernel, tq=tq),
        grid=(b, N_HEAD),
        in_specs=[seq, seq, seq] + extra_specs + slabs,
        out_specs=[seq] + slabs,
        out_shape=[jax.ShapeDtypeStruct(q.shape, F32)]
                  + [jax.ShapeDtypeStruct(w.shape, BF16) for w in to_bf16],
        scratch_shapes=[pltpu.VMEM((s, HEAD_DIM), BF16)] * 3,
        compiler_params=_params(blk, scratch, semantics=("arbitrary", "arbitrary")),
        name=name,
    )(q, k, v, *extra, *to_bf16)


def _fox_prompt(q, k, v, c, c_t, tq=256):
    s = q.shape[1]
    specs = [pl.BlockSpec((1, s, LANES), lambda bi, h: (bi, 0, 0)),
             pl.BlockSpec((1, N_HEAD, s), lambda bi, h: (bi, 0, 0))]
    return _prompt_attention(_fox_prompt_kernel, q, k, v, (c, c_t), specs, tq, "fox_prompt")[0]


def _sb_prompt(q, k, v, to_bf16, tq=256):
    return _prompt_attention(_sb_prompt_kernel, q, k, v, (), [], tq, "sb_prompt", to_bf16)


def _head_rows(page_ref, h, n):
    return page_ref[pl.ds(h, n, stride=N_HEAD), :]


def _stack_heads(rows8):
    t = 8
    return jnp.concatenate(
        [jnp.broadcast_to(rows8[h:h + 1, :], (t, LANES)) for h in range(N_HEAD)], axis=0)


def _decode_scores(q_ref, kn_ref, k_refs, page):
    t = q_ref.shape[1]
    q = q_ref[0]
    first = lax.broadcasted_iota(jnp.int32, (t, 2 * HEAD_DIM), 1) < HEAD_DIM
    qpair = []
    for pr in range(N_HEAD // 2):
        qq = q[:, pr * 2 * HEAD_DIM:(pr + 1) * 2 * HEAD_DIM]
        qpair.append(jnp.concatenate([jnp.where(first, qq, 0.0), jnp.where(first, 0.0, qq)],
                                     axis=0).astype(BF16))

    def scores(pair_keys):
        outs = [lax.dot_general(qpair[pr], pair_keys(pr), NT_DIMS, preferred_element_type=F32)
                for pr in range(N_HEAD // 2)]
        return jnp.concatenate(outs, axis=0) * SCALE

    chunks = []
    for kr in k_refs:
        chunks.append(scores(lambda pr: jnp.concatenate(
            [_head_rows(kr, 2 * pr, page), _head_rows(kr, 2 * pr + 1, page)],
            axis=1).astype(BF16)))
    kn = jnp.concatenate([kn_ref[0], jnp.zeros((page - t, W_MIX), F32)], axis=0).astype(BF16)
    chunks.append(scores(lambda pr: kn[:, pr * 2 * HEAD_DIM:(pr + 1) * 2 * HEAD_DIM]))
    return chunks


def _decode_apply(w_chunks, vn_ref, v_refs, page):
    t = vn_ref.shape[1]
    acc = [jnp.zeros((t, HEAD_DIM), F32) for _ in range(N_HEAD)]

    def apply(w, pair_values):
        for pr in range(N_HEAD // 2):
            o = jnp.dot(w[pr * 2 * t:(pr + 1) * 2 * t, :].astype(BF16), pair_values(pr),
                        preferred_element_type=F32)
            acc[2 * pr] = acc[2 * pr] + o[:t, :HEAD_DIM]
            acc[2 * pr + 1] = acc[2 * pr + 1] + o[t:, HEAD_DIM:]

    for w, vr in zip(w_chunks[:-1], v_refs):
        apply(w, lambda pr: jnp.concatenate(
            [_head_rows(vr, 2 * pr, page), _head_rows(vr, 2 * pr + 1, page)],
            axis=1).astype(BF16))
    vn = jnp.concatenate([vn_ref[0], jnp.zeros((page - t, W_MIX), F32)], axis=0).astype(BF16)
    apply(w_chunks[-1], lambda pr: vn[:, pr * 2 * HEAD_DIM:(pr + 1) * 2 * HEAD_DIM])
    return acc


def _fox_decode(tok_refs, page_refs, o_ref, *, page):
    q_ref, kn_ref, vn_ref, lfn_ref = tok_refs
    k_refs, v_refs, lf_refs = page_refs
    t = q_ref.shape[1]
    rows = N_HEAD * t

    tri3 = _tri_rhs(3, suffix=False)
    carry = jnp.zeros((N_HEAD, 1), F32)
    c_chunks = []
    for lr in lf_refs:
        loc = _prefix_lanes(lr[...], tri3)
        c_chunks.append(loc + carry)
        carry = carry + loc[:, LANES - 1:LANES]
    c_new = _prefix_lanes(lfn_ref[0], tri3) + carry
    c_chunks.append(c_new)

    sub = lax.broadcasted_iota(jnp.int32, (rows, LANES), 0) & (t - 1)
    lane = lax.broadcasted_iota(jnp.int32, (rows, LANES), 1)
    c_new_rows = _stack_heads(c_new)
    cq = jnp.sum(jnp.where(lane == sub, c_new_rows, 0.0), axis=1, keepdims=True)

    s_chunks = _decode_scores(q_ref, kn_ref, k_refs, page)
    s_chunks = [s + cq - _stack_heads(c) for s, c in zip(s_chunks, c_chunks)]
    s_chunks[-1] = jnp.where(lane <= sub, s_chunks[-1], -jnp.inf)

    mx = s_chunks[0]
    for s in s_chunks[1:]:
        mx = jnp.maximum(mx, s)
    m = jnp.max(mx, axis=1, keepdims=True)
    p_chunks = [jnp.exp(s - m) for s in s_chunks]
    tot = p_chunks[0]
    for p in p_chunks[1:]:
        tot = tot + p
    l = jnp.sum(tot, axis=1, keepdims=True)

    acc = _decode_apply(p_chunks, vn_ref, v_refs, page)
    o_ref[0] = jnp.concatenate(
        [acc[h] / l[h * t:(h + 1) * t, :] for h in range(N_HEAD)], axis=1)


def _sb_decode(tok_refs, page_refs, o_ref, *, page):
    q_ref, kn_ref, vn_ref = tok_refs
    k_refs, v_refs = page_refs
    t = q_ref.shape[1]
    rows = N_HEAD * t

    sub = lax.broadcasted_iota(jnp.int32, (rows, LANES), 0) & (t - 1)
    lane = lax.broadcasted_iota(jnp.int32, (rows, LANES), 1)
    strict_new = lane < sub

    z_chunks = _decode_scores(q_ref, kn_ref, k_refs, page)
    log_beta, log_1m = [], []
    for z in z_chunks:
        lb, l1 = _log_sigmoid_pair(z)
        log_beta.append(lb)
        log_1m.append(l1)
    log_1m[-1] = jnp.where(strict_new, log_1m[-1], 0.0)

    tri2 = _tri_rhs(2, suffix=True)
    later = jnp.zeros((rows, 1), F32)
    a_chunks = [None] * len(z_chunks)
    for c in range(len(z_chunks) - 1, -1, -1):
        a = jnp.exp(log_beta[c] + _suffix_lanes(log_1m[c], tri2) + later)
        if c == len(z_chunks) - 1:
            a = jnp.where(strict_new, a, 0.0)
        a_chunks[c] = a
        later = later + jnp.sum(log_1m[c], axis=1, keepdims=True)

    acc = _decode_apply(a_chunks, vn_ref, v_refs, page)
    o_ref[0] = jnp.concatenate(acc, axis=1)


def _mlp_decode_kernel(pt_ref, hn_ref, wu_ref, wd_ref, init_ref, g_ref, *refs, decode, normalize,
                       n_tok, n_cache):
    tok_refs = refs[:n_tok]
    cache_refs = refs[n_tok:n_tok + n_cache]
    y_ref, o_ref = refs[n_tok + n_cache:n_tok + n_cache + 2]
    bufs = refs[n_tok + n_cache + 2:n_tok + 2 * n_cache + 2]
    sem = refs[n_tok + 2 * n_cache + 2]
    n_pages = bufs[0].shape[1]
    n_steps = pl.num_programs(0) * pl.num_programs(1)
    step = pl.program_id(0) * pl.num_programs(1) + pl.program_id(1)
    slot = lax.rem(step, 2)

    def page_copies(seq, to_slot):
        return [pltpu.make_async_copy(cache.at[pt_ref[seq, p]], buf.at[to_slot, p],
                                      sem.at[to_slot, kind])
                for kind, (cache, buf) in enumerate(zip(cache_refs, bufs))
                for p in range(n_pages)]

    @pl.when(step == 0)
    def _():
        for copy in page_copies(0, 0):
            copy.start()

    following = jnp.minimum(step + 1, n_steps - 1)
    for copy in page_copies(following, 1 - slot):
        copy.start()
    _mlp_step(hn_ref, wu_ref, wd_ref, init_ref, g_ref, y_ref, normalize=normalize)
    for copy in page_copies(step, slot):
        copy.wait()
    decode(tok_refs, [[buf.at[slot, p] for p in range(n_pages)] for buf in bufs], o_ref)

    @pl.when(step == n_steps - 1)
    def _():
        for copy in page_copies(following, 1 - slot):
            copy.wait()


def _mlp_with_decode(name, hn, w_up_bf, w_down_bf, init, g_final, f_lo, n_f, normalize,
                     page_table, q, kn, vn, cache_k, cache_v, lf_new=None, cache_lf=None,
                     tm=512, tf=512):
    m, d = init.shape
    b, t, _ = q.shape
    n_i = m // tm
    assert n_i * n_f == b, "one decode sequence per MLP grid step"
    n_pages = page_table.shape[1]
    page = cache_k.shape[1] // N_HEAD
    seq = lambda i, f: i * n_f + f
    tok = pl.BlockSpec((1, t, W_MIX), lambda i, f, pt: (seq(i, f), 0, 0))
    in_hbm = pl.BlockSpec(memory_space=pl.ANY)

    in_specs = [pl.BlockSpec((tm, d), lambda i, f, pt: (i, 0)),
                pl.BlockSpec((d, tf), lambda i, f, pt: (0, f_lo + f)),
                pl.BlockSpec((tf, d), lambda i, f, pt: (f_lo + f, 0)),
                pl.BlockSpec((tm, d), lambda i, f, pt: (i, 0), pipeline_mode=pl.Buffered(1)),
                pl.BlockSpec((1, d), lambda i, f, pt: (0, 0)),
                tok, tok, tok]
    args = [hn, w_up_bf, w_down_bf, init, g_final.reshape(1, d), q, kn, vn]
    caches = [cache_k, cache_v]
    decode = _sb_decode
    if cache_lf is not None:
        decode = _fox_decode
        in_specs.append(pl.BlockSpec((1, N_HEAD, LANES), lambda i, f, pt: (seq(i, f), 0, 0)))
        args.append(lf_new)
        caches.append(cache_lf)
    n_tok = len(args) - 5
    page_bufs = [pltpu.VMEM((2, n_pages) + c.shape[1:], F32) for c in caches]
    double = (_nbytes((tm, d), BF16) + _nbytes((d, tf), BF16) + _nbytes((tf, d), BF16)
              + _nbytes((tm, d), F32) + (n_tok + 1) * _nbytes((t, W_MIX), F32))
    single = _nbytes((tm, d), F32) + sum(_nbytes(p.shape, F32) for p in page_bufs)
    return pl.pallas_call(
        functools.partial(_mlp_decode_kernel, normalize=normalize, n_tok=n_tok,
                          n_cache=len(caches), decode=functools.partial(decode, page=page)),
        grid_spec=pltpu.PrefetchScalarGridSpec(
            num_scalar_prefetch=1,
            grid=(n_i, n_f),
            in_specs=in_specs + [in_hbm] * len(caches),
            out_specs=[pl.BlockSpec((tm, d), lambda i, f, pt: (i, 0)),
                       pl.BlockSpec((1, t, W_MIX), lambda i, f, pt: (seq(i, f), 0, 0))],
            scratch_shapes=page_bufs + [pltpu.SemaphoreType.DMA((2, len(caches)))],
        ),
        out_shape=[jax.ShapeDtypeStruct((m, d), F32), jax.ShapeDtypeStruct(q.shape, F32)],
        compiler_params=_params(double, single, semantics=("arbitrary", "arbitrary")),
        name=name,
    )(page_table, *args, *caches)


MLP_TILE = 512


def kernel(x_prompt, x_sample, cache_fox_k, cache_fox_v, cache_fox_logf, cache_sb_k, cache_sb_v,
           page_table, g_mix, w_in, b_f, g_out_fox, g_out_sb, w_out, g_ffn, w_up, w_down, g_final):
    depth = w_in.shape[0]
    assert depth == 1, "single-layer trunk"
    b, s, d = x_prompt.shape
    db, t, _ = x_sample.shape
    n_pool, page = cache_fox_k.shape[1], cache_fox_k.shape[2]
    assert page == LANES and cache_fox_k.shape[3:] == (N_HEAD, HEAD_DIM)

    n_main = N_SEG * W_MIX
    w_in_bf = w_in[0, :, :n_main].astype(BF16)
    wg_bf = jnp.pad(w_in[0, :, n_main:], ((0, 0), (0, LANES - N_HEAD))).astype(BF16)
    bg = jnp.pad(b_f[0], (0, LANES - N_HEAD)).reshape(1, LANES).astype(F32)
    w_out_bf = w_out[0].astype(BF16)
    mix_out = (g_out_fox[0], g_out_sb[0])

    xp = x_prompt.reshape(b * s, d)
    (qf, kf, vf, qs, ks, vs), lf = _project(xp, g_mix[0], w_in_bf, wg_bf, bg)
    xs = x_sample.reshape(db * t, d)
    (qf_s, kf_s, vf_s, qs_s, ks_s, vs_s), lf_s = _project(xs, g_mix[0], w_in_bf, wg_bf, bg)

    as_seq = lambda a: a.reshape(b, s, a.shape[-1])
    c = _cumsum_seq(as_seq(lf))
    c_t = jnp.swapaxes(c[:, :, :N_HEAD], 1, 2)
    o_f = _fox_prompt(as_seq(qf), as_seq(kf), as_seq(vf), c, c_t)
    o_s, w_up_bf, w_down_bf = _sb_prompt(as_seq(qs), as_seq(ks), as_seq(vs),
                                         (w_up[0], w_down[0]))
    h, hn = _attn_out(o_f.reshape(b * s, W_MIX), o_s.reshape(b * s, W_MIX), *mix_out, xp,
                      w_out_bf, g_ffn[0])

    as_tok = lambda a: a.reshape(db, t, a.shape[-1])
    rows_view = lambda cache: cache[0].reshape(n_pool, page * N_HEAD, HEAD_DIM)
    lf_cache_t = jnp.swapaxes(cache_fox_logf[0], 1, 2)
    lf_new_t = jnp.swapaxes(as_tok(lf_s)[:, :, :N_HEAD], 1, 2)
    lf_new_t = jnp.pad(lf_new_t, ((0, 0), (0, 0), (0, LANES - t)))
    n_f = db // (b * s // MLP_TILE)
    assert 2 * n_f * MLP_TILE == w_up_bf.shape[1], "two halves cover the hidden width"
    mlp_in = (hn, w_up_bf, w_down_bf)
    y_half, o_f_s = _mlp_with_decode(
        "mlp_fox_decode", *mlp_in, h, g_final, 0, n_f, False, page_table,
        as_tok(qf_s), as_tok(kf_s), as_tok(vf_s), rows_view(cache_fox_k),
        rows_view(cache_fox_v), lf_new=lf_new_t, cache_lf=lf_cache_t,
        tm=MLP_TILE, tf=MLP_TILE)
    y_prompt, o_s_s = _mlp_with_decode(
        "mlp_sb_decode", *mlp_in, y_half, g_final, n_f, n_f, True, page_table,
        as_tok(qs_s), as_tok(ks_s), as_tok(vs_s), rows_view(cache_sb_k),
        rows_view(cache_sb_v), tm=MLP_TILE, tf=MLP_TILE)

    h_s, hn_s = _attn_out(o_f_s.reshape(db * t, W_MIX), o_s_s.reshape(db * t, W_MIX), *mix_out,
                          xs, w_out_bf, g_ffn[0])
    y_sample = _mlp(hn_s, w_up_bf, w_down_bf, h_s, g_final)

    heads = lambda a, n0, n1: a.reshape(1, n0, n1, N_HEAD, HEAD_DIM)
    return (y_prompt.reshape(b, s, d), y_sample.reshape(db, t, d),
            heads(kf, b, s), heads(vf, b, s), lf[:, :N_HEAD].reshape(1, b, s, N_HEAD),
            heads(ks, b, s), heads(vs, b, s),
            heads(kf_s, db, t), heads(vf_s, db, t), lf_s[:, :N_HEAD].reshape(1, db, t, N_HEAD),
            heads(ks_s, db, t), heads(vs_s, db, t))
```

```python
import functools

import jax
import jax.numpy as jnp
from jax import lax
from jax.experimental import pallas as pl
from jax.experimental.pallas import tpu as pltpu

F32 = jnp.float32
BF16 = jnp.bfloat16

EPS = 1e-6
HEAD_DIM = 128
N_HEAD = 8
W_MIX = N_HEAD * HEAD_DIM
SCALE = HEAD_DIM ** -0.5
LOG2E = 1.4426950408889634
LANES = 128
V7X_VMEM_BYTES = 64 * 1024 * 1024
COMPILER_TEMP_BYTES = 12 * 1024 * 1024

NT_DIMS = (((1,), (1,)), ((), ()))


def _vmem_limit(block_bytes, scratch_bytes=0):
    need = 2 * block_bytes + scratch_bytes + COMPILER_TEMP_BYTES
    return int(min(need, V7X_VMEM_BYTES - 4 * 1024 * 1024))


def _nbytes(shape, dtype):
    n = 1
    for s in shape:
        n *= s
    return n * jnp.dtype(dtype).itemsize


def _params(block_bytes, scratch_bytes=0, semantics=None):
    return pltpu.CompilerParams(
        dimension_semantics=semantics,
        vmem_limit_bytes=_vmem_limit(block_bytes, scratch_bytes))


def _rms(x, g):
    ms = jnp.mean(x * x, axis=-1, keepdims=True)
    return x * lax.rsqrt(ms + EPS) * g


def _log_sigmoid_pair(z):
    exp_neg_abs = jnp.exp2(jnp.abs(z) * (-LOG2E))
    log_sig = jnp.minimum(z, 0.0) - jnp.log(1.0 + exp_neg_abs)
    return log_sig, log_sig - z


def _tri_rhs(copies, suffix):
    r = lax.broadcasted_iota(jnp.int32, (copies * LANES, LANES), 0) & (LANES - 1)
    c = lax.broadcasted_iota(jnp.int32, (copies * LANES, LANES), 1)
    cond = (r > c) if suffix else (r <= c)
    return jnp.where(cond, 1.0, 0.0).astype(BF16)


def _split3(x):
    hi = x.astype(BF16)
    r = x - hi.astype(F32)
    mid = r.astype(BF16)
    lo = (r - mid.astype(F32)).astype(BF16)
    return hi, mid, lo


def _prefix_lanes(x, tri3):
    hi, mid, lo = _split3(x)
    return jnp.dot(jnp.concatenate([hi, mid, lo], axis=1), tri3,
                   preferred_element_type=F32)


def _suffix_lanes(x, tri2):
    hi = x.astype(BF16)
    lo = (x - hi.astype(F32)).astype(BF16)
    return jnp.dot(jnp.concatenate([hi, lo], axis=1), tri2,
                   preferred_element_type=F32)


N_SEG = 6


def _segment_of(i, j):
    return jnp.where(lax.rem(i, 2) == 0, j, N_SEG - 1 - j)


def _proj_kernel(x_ref, g_ref, w_ref, wg_ref, bg_ref, *refs):
    out_refs = refs[:N_SEG]
    gate_ref, gate_heads_ref, xn_ref = refs[N_SEG:]
    seg_now = _segment_of(pl.program_id(0), pl.program_id(1))

    @pl.when(pl.program_id(1) == 0)
    def _():
        xn = _rms(x_ref[...], g_ref[...]).astype(BF16)
        xn_ref[...] = xn
        y = jnp.dot(xn, wg_ref[...], preferred_element_type=F32) + bg_ref[...]
        gate = _log_sigmoid_pair(y)[0]
        gate_ref[...] = gate
        gate_heads_ref[...] = gate[:, :N_HEAD]

    for seg in range(N_SEG):
        @pl.when(seg_now == seg)
        def _(seg=seg):
            out_refs[seg][...] = jnp.dot(xn_ref[...], w_ref[...], preferred_element_type=F32)


def _project(x, g, w_bf, wg_bf, bg, tm=512):
    m, d = x.shape
    row = lambda i, j: (i, 0)
    const = lambda i, j: (0, 0)
    blk = (_nbytes((tm, d), F32) + _nbytes((d, W_MIX), BF16) + _nbytes((d, LANES), BF16)
           + N_SEG * _nbytes((tm, W_MIX), F32) + 2 * _nbytes((tm, LANES), F32))
    outs = pl.pallas_call(
        _proj_kernel,
        grid=(m // tm, N_SEG),
        in_specs=[pl.BlockSpec((tm, d), row),
                  pl.BlockSpec((1, d), const),
                  pl.BlockSpec((d, W_MIX), lambda i, j: (0, _segment_of(i, j))),
                  pl.BlockSpec((d, LANES), const),
                  pl.BlockSpec((1, LANES), const)],
        out_specs=[pl.BlockSpec((tm, W_MIX), row)] * N_SEG
                  + [pl.BlockSpec((tm, LANES), row), pl.BlockSpec((tm, N_HEAD), row)],
        out_shape=[jax.ShapeDtypeStruct((m, W_MIX), F32)] * N_SEG
                  + [jax.ShapeDtypeStruct((m, LANES), F32),
                     jax.ShapeDtypeStruct((m, N_HEAD), F32)],
        scratch_shapes=[pltpu.VMEM((tm, d), BF16)],
        compiler_params=_params(blk, _nbytes((tm, d), BF16),
                                semantics=("arbitrary", "arbitrary")),
        name="project",
    )(x, g.reshape(1, d), w_bf, wg_bf, bg)
    return outs[:N_SEG], outs[N_SEG], outs[N_SEG + 1]


def _attn_out_kernel(of_ref, os_ref, gf_ref, gs_ref, x_ref, w_ref, gn_ref, h_ref, hn_ref):
    nf = _rms(of_ref[...], gf_ref[...]).astype(BF16)
    ns = _rms(os_ref[...], gs_ref[...]).astype(BF16)
    proj = (jnp.dot(nf, w_ref[0:W_MIX, :], preferred_element_type=F32)
            + jnp.dot(ns, w_ref[W_MIX:2 * W_MIX, :], preferred_element_type=F32))
    h = x_ref[...] + proj
    h_ref[...] = h
    hn_ref[...] = _rms(h, gn_ref[...]).astype(BF16)


def _attn_out(o_f, o_s, g_f, g_s, x, w_out_bf, g_ffn, tm=512):
    m, d = x.shape
    blk = (2 * _nbytes((tm, W_MIX), F32) + 2 * _nbytes((tm, d), F32)
           + _nbytes((2 * W_MIX, d), BF16) + _nbytes((tm, d), BF16))
    row = lambda i: (i, 0)
    const = lambda i: (0, 0)
    return pl.pallas_call(
        _attn_out_kernel,
        grid=(m // tm,),
        in_specs=[pl.BlockSpec((tm, W_MIX), row), pl.BlockSpec((tm, W_MIX), row),
                  pl.BlockSpec((1, W_MIX), const), pl.BlockSpec((1, W_MIX), const),
                  pl.BlockSpec((tm, d), row),
                  pl.BlockSpec((2 * W_MIX, d), const),
                  pl.BlockSpec((1, d), const)],
        out_specs=[pl.BlockSpec((tm, d), row), pl.BlockSpec((tm, d), row)],
        out_shape=[jax.ShapeDtypeStruct((m, d), F32), jax.ShapeDtypeStruct((m, d), BF16)],
        compiler_params=_params(blk, semantics=("arbitrary",)),
        name="attn_out",
    )(o_f, o_s, g_f.reshape(1, -1), g_s.reshape(1, -1), x, w_out_bf, g_ffn.reshape(1, -1))


def _mlp_step(hn_ref, wu_ref, wd_ref, init_ref, g_ref, y_ref, *, normalize):
    f = pl.program_id(1)

    @pl.when(f == 0)
    def _():
        y_ref[...] = init_ref[...]

    u = jnp.maximum(jnp.dot(hn_ref[...], wu_ref[...], preferred_element_type=F32), 0.0)
    y_ref[...] += jnp.dot((u * u).astype(BF16), wd_ref[...], preferred_element_type=F32)

    if normalize:
        @pl.when(f == pl.num_programs(1) - 1)
        def _():
            y_ref[...] = _rms(y_ref[...], g_ref[...])


def _mlp_kernel(hn_ref, wu_ref, wd_ref, h_ref, g_ref, y_ref):
    _mlp_step(hn_ref, wu_ref, wd_ref, h_ref, g_ref, y_ref, normalize=True)


def _mlp(hn, w_up_bf, w_down_bf, h, g_final, tm=1024, tf=512):
    m, d = h.shape
    d_ff = w_up_bf.shape[1]
    blk = (_nbytes((tm, d), BF16) + _nbytes((d, tf), BF16) + _nbytes((tf, d), BF16)
           + 2 * _nbytes((tm, d), F32))
    return pl.pallas_call(
        _mlp_kernel,
        grid=(m // tm, d_ff // tf),
        in_specs=[pl.BlockSpec((tm, d), lambda i, f: (i, 0)),
                  pl.BlockSpec((d, tf), lambda i, f: (0, f)),
                  pl.BlockSpec((tf, d), lambda i, f: (f, 0)),
                  pl.BlockSpec((tm, d), lambda i, f: (i, 0)),
                  pl.BlockSpec((1, d), lambda i, f: (0, 0))],
        out_specs=pl.BlockSpec((tm, d), lambda i, f: (i, 0)),
        out_shape=jax.ShapeDtypeStruct((m, d), F32),
        compiler_params=_params(blk, semantics=("arbitrary", "arbitrary")),
        name="mlp",
    )(hn, w_up_bf, w_down_bf, h, g_final.reshape(1, d))


def _cumsum_seq_kernel(x_ref, o_ref, ot_ref):
    s = x_ref.shape[1]
    r = lax.broadcasted_iota(jnp.int32, (LANES, 3 * LANES), 0)
    c = lax.broadcasted_iota(jnp.int32, (LANES, 3 * LANES), 1) & (LANES - 1)
    tri3 = jnp.where(c <= r, 1.0, 0.0).astype(BF16)
    carry = jnp.zeros((1, LANES), F32)
    for ch in range(s // LANES):
        x = x_ref[0, ch * LANES:(ch + 1) * LANES, :]
        hi, mid, lo = _split3(x)
        loc = jnp.dot(tri3, jnp.concatenate([hi, mid, lo], axis=0),
                      preferred_element_type=F32)
        c = loc + carry
        o_ref[0, ch * LANES:(ch + 1) * LANES, :] = c
        ot_ref[0, :, ch * LANES:(ch + 1) * LANES] = c.T[:N_HEAD, :]
        carry = carry + loc[LANES - 1:LANES, :]


def _cumsum_seq(x):
    b, s, _ = x.shape
    blk = 2 * _nbytes((s, LANES), F32) + _nbytes((N_HEAD, s), F32)
    return pl.pallas_call(
        _cumsum_seq_kernel,
        grid=(b,),
        in_specs=[pl.BlockSpec((1, s, LANES), lambda i: (i, 0, 0))],
        out_specs=[pl.BlockSpec((1, s, LANES), lambda i: (i, 0, 0)),
                   pl.BlockSpec((1, N_HEAD, s), lambda i: (i, 0, 0))],
        out_shape=[jax.ShapeDtypeStruct(x.shape, F32),
                   jax.ShapeDtypeStruct((b, N_HEAD, s), F32)],
        compiler_params=_params(blk, semantics=("arbitrary",)),
        name="cumsum_seq",
    )(x)


def _fox_prompt_kernel(q_ref, k_ref, v_ref, c_ref, ct_ref, o_ref, qbf, kbf, vbf, *, tq):
    h = pl.program_id(1)
    s_len = q_ref.shape[1]
    qbf[...] = q_ref[0].astype(BF16)
    kbf[...] = k_ref[0].astype(BF16)
    vbf[...] = v_ref[0].astype(BF16)
    lane = lax.broadcasted_iota(jnp.int32, (tq, LANES), 1)
    qpos = lax.broadcasted_iota(jnp.int32, (tq, tq), 0)
    kpos = lax.broadcasted_iota(jnp.int32, (tq, tq), 1)
    causal = kpos <= qpos
    for i in range(s_len // tq):
        rows = slice(i * tq, (i + 1) * tq)
        q = qbf[rows, :]
        cq = jnp.sum(jnp.where(lane == h, c_ref[0, rows, :], 0.0), axis=1, keepdims=True)
        cq = cq * LOG2E
        m = jnp.full((tq, 1), -jnp.inf, F32)
        l = jnp.zeros((tq, 1), F32)
        acc = jnp.zeros((tq, HEAD_DIM), F32)
        for kb in [i] + list(range(i)):
            cols = slice(kb * tq, (kb + 1) * tq)
            s = lax.dot_general(q, kbf[cols, :], NT_DIMS, preferred_element_type=F32)
            s = s * (SCALE * LOG2E) + cq - ct_ref[0, pl.ds(h, 1), cols] * LOG2E
            if kb == i:
                s = jnp.where(causal, s, -jnp.inf)
            m_new = jnp.maximum(m, jnp.max(s, axis=1, keepdims=True))
            alpha = jnp.exp2(m - m_new)
            p = jnp.exp2(s - m_new)
            l = alpha * l + jnp.sum(p, axis=1, keepdims=True)
            acc = alpha * acc + jnp.dot(p.astype(BF16), vbf[cols, :],
                                        preferred_element_type=F32)
            m = m_new
        o_ref[0, rows, :] = acc / l


def _sb_prompt_kernel(q_ref, k_ref, v_ref, *refs, tq):
    n_cast = (len(refs) - 4) // 2
    o_ref = refs[n_cast]
    qbf, kbf, vbf = refs[2 * n_cast + 1:]
    for src, dst in zip(refs[:n_cast], refs[n_cast + 1:2 * n_cast + 1]):
        dst[...] = src[...].astype(BF16)
    s_len = q_ref.shape[1]
    qbf[...] = q_ref[0].astype(BF16)
    kbf[...] = k_ref[0].astype(BF16)
    vbf[...] = v_ref[0].astype(BF16)
    tri2 = _tri_rhs(2, suffix=True)
    qpos = lax.broadcasted_iota(jnp.int32, (tq, tq), 0)
    kpos = lax.broadcasted_iota(jnp.int32, (tq, tq), 1)
    strict = kpos < qpos
    for i in range(s_len // tq):
        rows = slice(i * tq, (i + 1) * tq)
        q = qbf[rows, :]
        later = jnp.zeros((tq, 1), F32)
        acc = jnp.zeros((tq, HEAD_DIM), F32)
        for kb in range(i, -1, -1):
            cols = slice(kb * tq, (kb + 1) * tq)
            z = lax.dot_general(q, kbf[cols, :], NT_DIMS, preferred_element_type=F32) * SCALE
            log_beta, log_1m = _log_sigmoid_pair(z)
            if kb == i:
                log_1m = jnp.where(strict, log_1m, 0.0)
            parts = [None] * (tq // LANES)
            for sub in range(tq // LANES - 1, -1, -1):
                ln = slice(sub * LANES, (sub + 1) * LANES)
                l1 = log_1m[:, ln]
                a = jnp.exp(log_beta[:, ln] + _suffix_lanes(l1, tri2) + later)
                if kb == i:
                    a = jnp.where(strict[:, ln], a, 0.0)
                parts[sub] = a.astype(BF16)
                later = later + jnp.sum(l1, axis=1, keepdims=True)
            acc = acc + jnp.dot(jnp.concatenate(parts, axis=1), vbf[cols, :],
                                preferred_element_type=F32)
        o_ref[0, rows, :] = acc


def _prompt_attention(kernel, q, k, v, extra, extra_specs, tq, name, to_bf16=()):
    b, s, _ = q.shape
    head = lambda bi, h: (bi, 0, h)
    blk = 4 * _nbytes((s, HEAD_DIM), F32) + _nbytes((s, LANES), F32) + _nbytes((N_HEAD, s), F32)
    scratch = 3 * _nbytes((s, HEAD_DIM), BF16)
    seq = pl.BlockSpec((1, s, HEAD_DIM), head)
    slabs = []
    for w in to_bf16:
        rows = w.shape[0] // (b * N_HEAD)
        assert rows * b * N_HEAD == w.shape[0], "one slab of rows per grid step"
        slabs.append(pl.BlockSpec((rows, w.shape[1]), lambda bi, h: (bi * N_HEAD + h, 0)))
        blk += _nbytes((rows, w.shape[1]), F32) + _nbytes((rows, w.shape[1]), BF16)
    return pl.pallas_call(
        functools.partial(kernel, tq=tq),
        grid=(b, N_HEAD),
        in_specs=[seq, seq, seq] + extra_specs + slabs,
        out_specs=[seq] + slabs,
        out_shape=[jax.ShapeDtypeStruct(q.shape, F32)]
                  + [jax.ShapeDtypeStruct(w.shape, BF16) for w in to_bf16],
        scratch_shapes=[pltpu.VMEM((s, HEAD_DIM), BF16)] * 3,
        compiler_params=_params(blk, scratch, semantics=("arbitrary", "arbitrary")),
        name=name,
    )(q, k, v, *extra, *to_bf16)


def _fox_prompt(q, k, v, c, c_t, tq=256):
    s = q.shape[1]
    specs = [pl.BlockSpec((1, s, LANES), lambda bi, h: (bi, 0, 0)),
             pl.BlockSpec((1, N_HEAD, s), lambda bi, h: (bi, 0, 0))]
    return _prompt_attention(_fox_prompt_kernel, q, k, v, (c, c_t), specs, tq, "fox_prompt")[0]


def _sb_prompt(q, k, v, to_bf16, tq=256):
    return _prompt_attention(_sb_prompt_kernel, q, k, v, (), [], tq, "sb_prompt", to_bf16)


def _head_rows(page_ref, h, n):
    return page_ref[pl.ds(h, n, stride=N_HEAD), :]


def _stack_heads(rows8):
    t = 8
    return jnp.concatenate(
        [jnp.broadcast_to(rows8[h:h + 1, :], (t, LANES)) for h in range(N_HEAD)], axis=0)


def _decode_scores(q_ref, kn_ref, k_refs, page):
    t = q_ref.shape[1]
    q = q_ref[0]
    first = lax.broadcasted_iota(jnp.int32, (t, 2 * HEAD_DIM), 1) < HEAD_DIM
    qpair = []
    for pr in range(N_HEAD // 2):
        qq = q[:, pr * 2 * HEAD_DIM:(pr + 1) * 2 * HEAD_DIM]
        qpair.append(jnp.concatenate([jnp.where(first, qq, 0.0), jnp.where(first, 0.0, qq)],
                                     axis=0).astype(BF16))

    def scores(pair_keys):
        outs = [lax.dot_general(qpair[pr], pair_keys(pr), NT_DIMS, preferred_element_type=F32)
                for pr in range(N_HEAD // 2)]
        return jnp.concatenate(outs, axis=0) * SCALE

    chunks = []
    for kr in k_refs:
        chunks.append(scores(lambda pr: jnp.concatenate(
            [_head_rows(kr, 2 * pr, page), _head_rows(kr, 2 * pr + 1, page)],
            axis=1).astype(BF16)))
    kn = jnp.concatenate([kn_ref[0], jnp.zeros((page - t, W_MIX), F32)], axis=0).astype(BF16)
    chunks.append(scores(lambda pr: kn[:, pr * 2 * HEAD_DIM:(pr + 1) * 2 * HEAD_DIM]))
    return chunks


def _decode_apply(w_chunks, vn_ref, v_refs, page):
    t = vn_ref.shape[1]
    acc = [jnp.zeros((t, HEAD_DIM), F32) for _ in range(N_HEAD)]

    def apply(w, pair_values):
        for pr in range(N_HEAD // 2):
            o = jnp.dot(w[pr * 2 * t:(pr + 1) * 2 * t, :].astype(BF16), pair_values(pr),
                        preferred_element_type=F32)
            acc[2 * pr] = acc[2 * pr] + o[:t, :HEAD_DIM]
            acc[2 * pr + 1] = acc[2 * pr + 1] + o[t:, HEAD_DIM:]

    for w, vr in zip(w_chunks[:-1], v_refs):
        apply(w, lambda pr: jnp.concatenate(
            [_head_rows(vr, 2 * pr, page), _head_rows(vr, 2 * pr + 1, page)],
            axis=1).astype(BF16))
    vn = jnp.concatenate([vn_ref[0], jnp.zeros((page - t, W_MIX), F32)], axis=0).astype(BF16)
    apply(w_chunks[-1], lambda pr: vn[:, pr * 2 * HEAD_DIM:(pr + 1) * 2 * HEAD_DIM])
    return acc


def _fox_decode(tok_refs, page_refs, o_ref, *, page):
    q_ref, kn_ref, vn_ref, lfn_ref = tok_refs
    k_refs, v_refs, lf_refs = page_refs
    t = q_ref.shape[1]
    rows = N_HEAD * t

    tri3 = _tri_rhs(3, suffix=False)
    carry = jnp.zeros((N_HEAD, 1), F32)
    c_chunks = []
    for lr in lf_refs:
        loc = _prefix_lanes(lr[...], tri3)
        c_chunks.append(loc + carry)
        carry = carry + loc[:, LANES - 1:LANES]
    c_new = _prefix_lanes(lfn_ref[0], tri3) + carry
    c_chunks.append(c_new)

    sub = lax.broadcasted_iota(jnp.int32, (rows, LANES), 0) & (t - 1)
    lane = lax.broadcasted_iota(jnp.int32, (rows, LANES), 1)
    c_new_rows = _stack_heads(c_new)
    cq = jnp.sum(jnp.where(lane == sub, c_new_rows, 0.0), axis=1, keepdims=True)

    s_chunks = _decode_scores(q_ref, kn_ref, k_refs, page)
    s_chunks = [s + cq - _stack_heads(c) for s, c in zip(s_chunks, c_chunks)]
    s_chunks[-1] = jnp.where(lane <= sub, s_chunks[-1], -jnp.inf)

    mx = s_chunks[0]
    for s in s_chunks[1:]:
        mx = jnp.maximum(mx, s)
    m = jnp.max(mx, axis=1, keepdims=True)
    p_chunks = [jnp.exp(s - m) for s in s_chunks]
    tot = p_chunks[0]
    for p in p_chunks[1:]:
        tot = tot + p
    l = jnp.sum(tot, axis=1, keepdims=True)

    acc = _decode_apply(p_chunks, vn_ref, v_refs, page)
    o_ref[0] = jnp.concatenate(
        [acc[h] / l[h * t:(h + 1) * t, :] for h in range(N_HEAD)], axis=1)


def _sb_decode(tok_refs, page_refs, o_ref, *, page):
    q_ref, kn_ref, vn_ref = tok_refs
    k_refs, v_refs = page_refs
    t = q_ref.shape[1]
    rows = N_HEAD * t

    sub = lax.broadcasted_iota(jnp.int32, (rows, LANES), 0) & (t - 1)
    lane = lax.broadcasted_iota(jnp.int32, (rows, LANES), 1)
    strict_new = lane < sub

    z_chunks = _decode_scores(q_ref, kn_ref, k_refs, page)
    log_beta, log_1m = [], []
    for z in z_chunks:
        lb, l1 = _log_sigmoid_pair(z)
        log_beta.append(lb)
        log_1m.append(l1)
    log_1m[-1] = jnp.where(strict_new, log_1m[-1], 0.0)

    tri2 = _tri_rhs(2, suffix=True)
    later = jnp.zeros((rows, 1), F32)
    a_chunks = [None] * len(z_chunks)
    for c in range(len(z_chunks) - 1, -1, -1):
        a = jnp.exp(log_beta[c] + _suffix_lanes(log_1m[c], tri2) + later)
        if c == len(z_chunks) - 1:
            a = jnp.where(strict_new, a, 0.0)
        a_chunks[c] = a
        later = later + jnp.sum(log_1m[c], axis=1, keepdims=True)

    acc = _decode_apply(a_chunks, vn_ref, v_refs, page)
    o_ref[0] = jnp.concatenate(acc, axis=1)


def _mlp_decode_kernel(pt_ref, hn_ref, wu_ref, wd_ref, init_ref, g_ref, *refs, decode, normalize,
                       n_tok, n_cache):
    tok_refs = refs[:n_tok]
    cache_refs = refs[n_tok:n_tok + n_cache]
    y_ref, o_ref = refs[n_tok + n_cache:n_tok + n_cache + 2]
    bufs = refs[n_tok + n_cache + 2:n_tok + 2 * n_cache + 2]
    sem = refs[n_tok + 2 * n_cache + 2]
    n_pages = bufs[0].shape[1]
    n_steps = pl.num_programs(0) * pl.num_programs(1)
    step = pl.program_id(0) * pl.num_programs(1) + pl.program_id(1)
    slot = lax.rem(step, 2)

    def page_copies(seq, to_slot):
        return [pltpu.make_async_copy(cache.at[pt_ref[seq, p]], buf.at[to_slot, p],
                                      sem.at[to_slot, kind])
                for kind, (cache, buf) in enumerate(zip(cache_refs, bufs))
                for p in range(n_pages)]

    @pl.when(step == 0)
    def _():
        for copy in page_copies(0, 0):
            copy.start()

    following = jnp.minimum(step + 1, n_steps - 1)
    for copy in page_copies(following, 1 - slot):
        copy.start()
    _mlp_step(hn_ref, wu_ref, wd_ref, init_ref, g_ref, y_ref, normalize=normalize)
    for copy in page_copies(step, slot):
        copy.wait()
    decode(tok_refs, [[buf.at[slot, p] for p in range(n_pages)] for buf in bufs], o_ref)

    @pl.when(step == n_steps - 1)
    def _():
        for copy in page_copies(following, 1 - slot):
            copy.wait()


def _mlp_with_decode(name, hn, w_up_bf, w_down_bf, init, g_final, f_lo, n_f, normalize,
                     page_table, q, kn, vn, cache_k, cache_v, lf_new=None, cache_lf=None,
                     tm=512, tf=512):
    m, d = init.shape
    b, t, _ = q.shape
    n_i = m // tm
    assert n_i * n_f == b, "one decode sequence per MLP grid step"
    n_pages = page_table.shape[1]
    page = cache_k.shape[1] // N_HEAD
    seq = lambda i, f: i * n_f + f
    tok = pl.BlockSpec((1, t, W_MIX), lambda i, f, pt: (seq(i, f), 0, 0))
    in_hbm = pl.BlockSpec(memory_space=pl.ANY)

    in_specs = [pl.BlockSpec((tm, d), lambda i, f, pt: (i, 0)),
                pl.BlockSpec((d, tf), lambda i, f, pt: (0, f_lo + f)),
                pl.BlockSpec((tf, d), lambda i, f, pt: (f_lo + f, 0)),
                pl.BlockSpec((tm, d), lambda i, f, pt: (i, 0), pipeline_mode=pl.Buffered(1)),
                pl.BlockSpec((1, d), lambda i, f, pt: (0, 0)),
                tok, tok, tok]
    args = [hn, w_up_bf, w_down_bf, init, g_final.reshape(1, d), q, kn, vn]
    caches = [cache_k, cache_v]
    decode = _sb_decode
    if cache_lf is not None:
        decode = _fox_decode
        in_specs.append(pl.BlockSpec((1, N_HEAD, LANES), lambda i, f, pt: (seq(i, f), 0, 0)))
        args.append(lf_new)
        caches.append(cache_lf)
    n_tok = len(args) - 5
    page_bufs = [pltpu.VMEM((2, n_pages) + c.shape[1:], F32) for c in caches]
    double = (_nbytes((tm, d), BF16) + _nbytes((d, tf), BF16) + _nbytes((tf, d), BF16)
              + _nbytes((tm, d), F32) + (n_tok + 1) * _nbytes((t, W_MIX), F32))
    single = _nbytes((tm, d), F32) + sum(_nbytes(p.shape, F32) for p in page_bufs)
    return pl.pallas_call(
        functools.partial(_mlp_decode_kernel, normalize=normalize, n_tok=n_tok,
                          n_cache=len(caches), decode=functools.partial(decode, page=page)),
        grid_spec=pltpu.PrefetchScalarGridSpec(
            num_scalar_prefetch=1,
            grid=(n_i, n_f),
            in_specs=in_specs + [in_hbm] * len(caches),
            out_specs=[pl.BlockSpec((tm, d), lambda i, f, pt: (i, 0)),
                       pl.BlockSpec((1, t, W_MIX), lambda i, f, pt: (seq(i, f), 0, 0))],
            scratch_shapes=page_bufs + [pltpu.SemaphoreType.DMA((2, len(caches)))],
        ),
        out_shape=[jax.ShapeDtypeStruct((m, d), F32), jax.ShapeDtypeStruct(q.shape, F32)],
        compiler_params=_params(double, single, semantics=("arbitrary", "arbitrary")),
        name=name,
    )(page_table, *args, *caches)


MLP_TILE = 512


def kernel(x_prompt, x_sample, cache_fox_k, cache_fox_v, cache_fox_logf, cache_sb_k, cache_sb_v,
           page_table, g_mix, w_in, b_f, g_out_fox, g_out_sb, w_out, g_ffn, w_up, w_down, g_final):
    depth = w_in.shape[0]
    assert depth == 1, "single-layer trunk"
    b, s, d = x_prompt.shape
    db, t, _ = x_sample.shape
    n_pool, page = cache_fox_k.shape[1], cache_fox_k.shape[2]
    assert page == LANES and cache_fox_k.shape[3:] == (N_HEAD, HEAD_DIM)

    n_main = N_SEG * W_MIX
    w_in_bf = w_in[0, :, :n_main].astype(BF16)
    wg_bf = jnp.pad(w_in[0, :, n_main:], ((0, 0), (0, LANES - N_HEAD))).astype(BF16)
    bg = jnp.pad(b_f[0], (0, LANES - N_HEAD)).reshape(1, LANES).astype(F32)
    w_out_bf = w_out[0].astype(BF16)
    mix_out = (g_out_fox[0], g_out_sb[0])

    xp = x_prompt.reshape(b * s, d)
    (qf, kf, vf, qs, ks, vs), lf, lf_heads = _project(xp, g_mix[0], w_in_bf, wg_bf, bg)
    xs = x_sample.reshape(db * t, d)
    (qf_s, kf_s, vf_s, qs_s, ks_s, vs_s), lf_s, lf_heads_s = _project(xs, g_mix[0], w_in_bf,
                                                                      wg_bf, bg)

    as_seq = lambda a: a.reshape(b, s, a.shape[-1])
    c, c_t = _cumsum_seq(as_seq(lf))
    o_f = _fox_prompt(as_seq(qf), as_seq(kf), as_seq(vf), c, c_t)
    o_s, w_up_bf, w_down_bf = _sb_prompt(as_seq(qs), as_seq(ks), as_seq(vs),
                                         (w_up[0], w_down[0]))
    h, hn = _attn_out(o_f.reshape(b * s, W_MIX), o_s.reshape(b * s, W_MIX), *mix_out, xp,
                      w_out_bf, g_ffn[0])

    as_tok = lambda a: a.reshape(db, t, a.shape[-1])
    rows_view = lambda cache: cache[0].reshape(n_pool, page * N_HEAD, HEAD_DIM)
    lf_cache_t = jnp.swapaxes(cache_fox_logf[0], 1, 2)
    lf_new_t = jnp.swapaxes(lf_heads_s.reshape(db, t, N_HEAD), 1, 2)
    lf_new_t = jnp.pad(lf_new_t, ((0, 0), (0, 0), (0, LANES - t)))
    n_f = db // (b * s // MLP_TILE)
    assert 2 * n_f * MLP_TILE == w_up_bf.shape[1], "two halves cover the hidden width"
    mlp_in = (hn, w_up_bf, w_down_bf)
    y_half, o_f_s = _mlp_with_decode(
        "mlp_fox_decode", *mlp_in, h, g_final, 0, n_f, False, page_table,
        as_tok(qf_s), as_tok(kf_s), as_tok(vf_s), rows_view(cache_fox_k),
        rows_view(cache_fox_v), lf_new=lf_new_t, cache_lf=lf_cache_t,
        tm=MLP_TILE, tf=MLP_TILE)
    y_prompt, o_s_s = _mlp_with_decode(
        "mlp_sb_decode", *mlp_in, y_half, g_final, n_f, n_f, True, page_table,
        as_tok(qs_s), as_tok(ks_s), as_tok(vs_s), rows_view(cache_sb_k),
        rows_view(cache_sb_v), tm=MLP_TILE, tf=MLP_TILE)

    h_s, hn_s = _attn_out(o_f_s.reshape(db * t, W_MIX), o_s_s.reshape(db * t, W_MIX), *mix_out,
                          xs, w_out_bf, g_ffn[0])
    y_sample = _mlp(hn_s, w_up_bf, w_down_bf, h_s, g_final)

    heads = lambda a, n0, n1: a.reshape(1, n0, n1, N_HEAD, HEAD_DIM)
    return (y_prompt.reshape(b, s, d), y_sample.reshape(db, t, d),
            heads(kf, b, s), heads(vf, b, s), lf_heads.reshape(1, b, s, N_HEAD),
            heads(ks, b, s), heads(vs, b, s),
            heads(kf_s, db, t), heads(vf_s, db, t), lf_heads_s.reshape(1, db, t, N_HEAD),
            heads(ks_s, db, t), heads(vs_s, db, t))
```

```python
import functools

import jax
import jax.numpy as jnp
from jax import lax
from jax.experimental import pallas as pl
from jax.experimental.pallas import tpu as pltpu

F32 = jnp.float32
BF16 = jnp.bfloat16

EPS = 1e-6
HEAD_DIM = 128
N_HEAD = 8
W_MIX = N_HEAD * HEAD_DIM
SCALE = HEAD_DIM ** -0.5
LOG2E = 1.4426950408889634
LANES = 128
V7X_VMEM_BYTES = 64 * 1024 * 1024
COMPILER_TEMP_BYTES = 12 * 1024 * 1024

NT_DIMS = (((1,), (1,)), ((), ()))


def _vmem_limit(block_bytes, scratch_bytes=0):
    need = 2 * block_bytes + scratch_bytes + COMPILER_TEMP_BYTES
    return int(min(need, V7X_VMEM_BYTES - 4 * 1024 * 1024))


def _nbytes(shape, dtype):
    n = 1
    for s in shape:
        n *= s
    return n * jnp.dtype(dtype).itemsize


def _params(block_bytes, scratch_bytes=0, semantics=None):
    return pltpu.CompilerParams(
        dimension_semantics=semantics,
        vmem_limit_bytes=_vmem_limit(block_bytes, scratch_bytes))


def _rms(x, g):
    ms = jnp.mean(x * x, axis=-1, keepdims=True)
    return x * lax.rsqrt(ms + EPS) * g


def _log_sigmoid_pair(z):
    exp_neg_abs = jnp.exp2(jnp.abs(z) * (-LOG2E))
    log_sig = jnp.minimum(z, 0.0) - jnp.log(1.0 + exp_neg_abs)
    return log_sig, log_sig - z


def _tri_rhs(copies, suffix):
    r = lax.broadcasted_iota(jnp.int32, (copies * LANES, LANES), 0) & (LANES - 1)
    c = lax.broadcasted_iota(jnp.int32, (copies * LANES, LANES), 1)
    cond = (r > c) if suffix else (r <= c)
    return jnp.where(cond, 1.0, 0.0).astype(BF16)


def _split3(x):
    hi = x.astype(BF16)
    r = x - hi.astype(F32)
    mid = r.astype(BF16)
    lo = (r - mid.astype(F32)).astype(BF16)
    return hi, mid, lo


def _prefix_lanes(x, tri3):
    hi, mid, lo = _split3(x)
    return jnp.dot(jnp.concatenate([hi, mid, lo], axis=1), tri3,
                   preferred_element_type=F32)


def _suffix_lanes(x, tri2):
    hi = x.astype(BF16)
    lo = (x - hi.astype(F32)).astype(BF16)
    return jnp.dot(jnp.concatenate([hi, lo], axis=1), tri2,
                   preferred_element_type=F32)


N_SEG = 6


def _segment_of(i, j):
    return jnp.where(lax.rem(i, 2) == 0, j, N_SEG - 1 - j)


def _proj_kernel(x_ref, g_ref, w_ref, wg_ref, bg_ref, *refs):
    out_refs = refs[:N_SEG]
    gate_ref, gate_heads_ref, xn_ref = refs[N_SEG:]
    seg_now = _segment_of(pl.program_id(0), pl.program_id(1))

    @pl.when(pl.program_id(1) == 0)
    def _():
        xn = _rms(x_ref[...], g_ref[...]).astype(BF16)
        xn_ref[...] = xn
        y = jnp.dot(xn, wg_ref[...], preferred_element_type=F32) + bg_ref[...]
        gate = _log_sigmoid_pair(y)[0]
        gate_ref[...] = gate
        gate_heads_ref[...] = gate[:, :N_HEAD]

    for seg in range(N_SEG):
        @pl.when(seg_now == seg)
        def _(seg=seg):
            out_refs[seg][...] = jnp.dot(xn_ref[...], w_ref[...], preferred_element_type=F32)


def _project(x, g, w_bf, wg_bf, bg, tm=512):
    m, d = x.shape
    row = lambda i, j: (i, 0)
    const = lambda i, j: (0, 0)
    blk = (_nbytes((tm, d), F32) + _nbytes((d, W_MIX), BF16) + _nbytes((d, LANES), BF16)
           + N_SEG * _nbytes((tm, W_MIX), F32) + 2 * _nbytes((tm, LANES), F32))
    outs = pl.pallas_call(
        _proj_kernel,
        grid=(m // tm, N_SEG),
        in_specs=[pl.BlockSpec((tm, d), row),
                  pl.BlockSpec((1, d), const),
                  pl.BlockSpec((d, W_MIX), lambda i, j: (0, _segment_of(i, j))),
                  pl.BlockSpec((d, LANES), const),
                  pl.BlockSpec((1, LANES), const)],
        out_specs=[pl.BlockSpec((tm, W_MIX), row)] * N_SEG
                  + [pl.BlockSpec((tm, LANES), row), pl.BlockSpec((tm, N_HEAD), row)],
        out_shape=[jax.ShapeDtypeStruct((m, W_MIX), F32)] * N_SEG
                  + [jax.ShapeDtypeStruct((m, LANES), F32),
                     jax.ShapeDtypeStruct((m, N_HEAD), F32)],
        scratch_shapes=[pltpu.VMEM((tm, d), BF16)],
        compiler_params=_params(blk, _nbytes((tm, d), BF16),
                                semantics=("arbitrary", "arbitrary")),
        name="project",
    )(x, g.reshape(1, d), w_bf, wg_bf, bg)
    return outs[:N_SEG], outs[N_SEG], outs[N_SEG + 1]


def _attn_out_kernel(of_ref, os_ref, gf_ref, gs_ref, x_ref, w_ref, gn_ref, h_ref, hn_ref):
    nf = _rms(of_ref[...], gf_ref[...]).astype(BF16)
    ns = _rms(os_ref[...], gs_ref[...]).astype(BF16)
    proj = (jnp.dot(nf, w_ref[0:W_MIX, :], preferred_element_type=F32)
            + jnp.dot(ns, w_ref[W_MIX:2 * W_MIX, :], preferred_element_type=F32))
    h = x_ref[...] + proj
    h_ref[...] = h
    hn_ref[...] = _rms(h, gn_ref[...]).astype(BF16)


def _attn_out(o_f, o_s, g_f, g_s, x, w_out_bf, g_ffn, tm=512):
    m, d = x.shape
    blk = (2 * _nbytes((tm, W_MIX), F32) + 2 * _nbytes((tm, d), F32)
           + _nbytes((2 * W_MIX, d), BF16) + _nbytes((tm, d), BF16))
    row = lambda i: (i, 0)
    const = lambda i: (0, 0)
    return pl.pallas_call(
        _attn_out_kernel,
        grid=(m // tm,),
        in_specs=[pl.BlockSpec((tm, W_MIX), row), pl.BlockSpec((tm, W_MIX), row),
                  pl.BlockSpec((1, W_MIX), const), pl.BlockSpec((1, W_MIX), const),
                  pl.BlockSpec((tm, d), row),
                  pl.BlockSpec((2 * W_MIX, d), const),
                  pl.BlockSpec((1, d), const)],
        out_specs=[pl.BlockSpec((tm, d), row), pl.BlockSpec((tm, d), row)],
        out_shape=[jax.ShapeDtypeStruct((m, d), F32), jax.ShapeDtypeStruct((m, d), BF16)],
        compiler_params=_params(blk, semantics=("arbitrary",)),
        name="attn_out",
    )(o_f, o_s, g_f.reshape(1, -1), g_s.reshape(1, -1), x, w_out_bf, g_ffn.reshape(1, -1))


def _mlp_step(hn_ref, wu_ref, wd_ref, init_ref, g_ref, y_ref, *, normalize):
    f = pl.program_id(1)

    @pl.when(f == 0)
    def _():
        y_ref[...] = init_ref[...]

    u = jnp.maximum(jnp.dot(hn_ref[...], wu_ref[...], preferred_element_type=F32), 0.0)
    y_ref[...] += jnp.dot((u * u).astype(BF16), wd_ref[...], preferred_element_type=F32)

    if normalize:
        @pl.when(f == pl.num_programs(1) - 1)
        def _():
            y_ref[...] = _rms(y_ref[...], g_ref[...])


def _mlp_kernel(hn_ref, wu_ref, wd_ref, h_ref, g_ref, y_ref):
    _mlp_step(hn_ref, wu_ref, wd_ref, h_ref, g_ref, y_ref, normalize=True)


def _mlp(hn, w_up_bf, w_down_bf, h, g_final, tm=1024, tf=512):
    m, d = h.shape
    d_ff = w_up_bf.shape[1]
    blk = (_nbytes((tm, d), BF16) + _nbytes((d, tf), BF16) + _nbytes((tf, d), BF16)
           + 2 * _nbytes((tm, d), F32))
    return pl.pallas_call(
        _mlp_kernel,
        grid=(m // tm, d_ff // tf),
        in_specs=[pl.BlockSpec((tm, d), lambda i, f: (i, 0)),
                  pl.BlockSpec((d, tf), lambda i, f: (0, f)),
                  pl.BlockSpec((tf, d), lambda i, f: (f, 0)),
                  pl.BlockSpec((tm, d), lambda i, f: (i, 0)),
                  pl.BlockSpec((1, d), lambda i, f: (0, 0))],
        out_specs=pl.BlockSpec((tm, d), lambda i, f: (i, 0)),
        out_shape=jax.ShapeDtypeStruct((m, d), F32),
        compiler_params=_params(blk, semantics=("arbitrary", "arbitrary")),
        name="mlp",
    )(hn, w_up_bf, w_down_bf, h, g_final.reshape(1, d))


def _cumsum_seq_kernel(x_ref, o_ref, ot_ref):
    s = x_ref.shape[1]
    r = lax.broadcasted_iota(jnp.int32, (LANES, 3 * LANES), 0)
    c = lax.broadcasted_iota(jnp.int32, (LANES, 3 * LANES), 1) & (LANES - 1)
    tri3 = jnp.where(c <= r, 1.0, 0.0).astype(BF16)
    carry = jnp.zeros((1, LANES), F32)
    for ch in range(s // LANES):
        x = x_ref[0, ch * LANES:(ch + 1) * LANES, :]
        hi, mid, lo = _split3(x)
        loc = jnp.dot(tri3, jnp.concatenate([hi, mid, lo], axis=0),
                      preferred_element_type=F32)
        c = loc + carry
        o_ref[0, ch * LANES:(ch + 1) * LANES, :] = c
        ot_ref[0, :, ch * LANES:(ch + 1) * LANES] = c.T[:N_HEAD, :]
        carry = carry + loc[LANES - 1:LANES, :]


def _cumsum_seq(x):
    b, s, _ = x.shape
    blk = 2 * _nbytes((s, LANES), F32) + _nbytes((N_HEAD, s), F32)
    return pl.pallas_call(
        _cumsum_seq_kernel,
        grid=(b,),
        in_specs=[pl.BlockSpec((1, s, LANES), lambda i: (i, 0, 0))],
        out_specs=[pl.BlockSpec((1, s, LANES), lambda i: (i, 0, 0)),
                   pl.BlockSpec((1, N_HEAD, s), lambda i: (i, 0, 0))],
        out_shape=[jax.ShapeDtypeStruct(x.shape, F32),
                   jax.ShapeDtypeStruct((b, N_HEAD, s), F32)],
        compiler_params=_params(blk, semantics=("arbitrary",)),
        name="cumsum_seq",
    )(x)


def _fox_prompt_kernel(q_ref, k_ref, v_ref, c_ref, ct_ref, o_ref, qbf, kbf, vbf, *, tq):
    h = pl.program_id(1)
    s_len = q_ref.shape[1]
    qbf[...] = q_ref[0].astype(BF16)
    kbf[...] = k_ref[0].astype(BF16)
    vbf[...] = v_ref[0].astype(BF16)
    lane = lax.broadcasted_iota(jnp.int32, (tq, LANES), 1)
    qpos = lax.broadcasted_iota(jnp.int32, (tq, tq), 0)
    kpos = lax.broadcasted_iota(jnp.int32, (tq, tq), 1)
    causal = kpos <= qpos
    for i in range(s_len // tq):
        rows = slice(i * tq, (i + 1) * tq)
        q = qbf[rows, :]
        cq = jnp.sum(jnp.where(lane == h, c_ref[0, rows, :], 0.0), axis=1, keepdims=True)
        cq = cq * LOG2E
        m = jnp.full((tq, 1), -jnp.inf, F32)
        l = jnp.zeros((tq, 1), F32)
        acc = jnp.zeros((tq, HEAD_DIM), F32)
        for kb in [i] + list(range(i)):
            cols = slice(kb * tq, (kb + 1) * tq)
            s = lax.dot_general(q, kbf[cols, :], NT_DIMS, preferred_element_type=F32)
            s = s * (SCALE * LOG2E) + cq - ct_ref[0, pl.ds(h, 1), cols] * LOG2E
            if kb == i:
                s = jnp.where(causal, s, -jnp.inf)
            m_new = jnp.maximum(m, jnp.max(s, axis=1, keepdims=True))
            alpha = jnp.exp2(m - m_new)
            p = jnp.exp2(s - m_new)
            l = alpha * l + jnp.sum(p, axis=1, keepdims=True)
            acc = alpha * acc + jnp.dot(p.astype(BF16), vbf[cols, :],
                                        preferred_element_type=F32)
            m = m_new
        o_ref[0, rows, :] = acc / l


def _sb_prompt_kernel(q_ref, k_ref, v_ref, *refs, tq):
    n_cast = (len(refs) - 4) // 2
    o_ref = refs[n_cast]
    qbf, kbf, vbf = refs[2 * n_cast + 1:]
    for src, dst in zip(refs[:n_cast], refs[n_cast + 1:2 * n_cast + 1]):
        dst[...] = src[...].astype(BF16)
    s_len = q_ref.shape[1]
    qbf[...] = q_ref[0].astype(BF16)
    kbf[...] = k_ref[0].astype(BF16)
    vbf[...] = v_ref[0].astype(BF16)
    tri2 = _tri_rhs(2, suffix=True)
    qpos = lax.broadcasted_iota(jnp.int32, (tq, tq), 0)
    kpos = lax.broadcasted_iota(jnp.int32, (tq, tq), 1)
    strict = kpos < qpos
    for i in range(s_len // tq):
        rows = slice(i * tq, (i + 1) * tq)
        q = qbf[rows, :]
        later = jnp.zeros((tq, 1), F32)
        acc = jnp.zeros((tq, HEAD_DIM), F32)
        for kb in range(i, -1, -1):
            cols = slice(kb * tq, (kb + 1) * tq)
            z = lax.dot_general(q, kbf[cols, :], NT_DIMS, preferred_element_type=F32) * SCALE
            log_beta, log_1m = _log_sigmoid_pair(z)
            if kb == i:
                log_1m = jnp.where(strict, log_1m, 0.0)
            parts = [None] * (tq // LANES)
            for sub in range(tq // LANES - 1, -1, -1):
                ln = slice(sub * LANES, (sub + 1) * LANES)
                l1 = log_1m[:, ln]
                a = jnp.exp(log_beta[:, ln] + _suffix_lanes(l1, tri2) + later)
                if kb == i:
                    a = jnp.where(strict[:, ln], a, 0.0)
                parts[sub] = a.astype(BF16)
                later = later + jnp.sum(l1, axis=1, keepdims=True)
            acc = acc + jnp.dot(jnp.concatenate(parts, axis=1), vbf[cols, :],
                                preferred_element_type=F32)
        o_ref[0, rows, :] = acc


def _prompt_attention(kernel, q, k, v, extra, extra_specs, tq, name, to_bf16=()):
    b, s, _ = q.shape
    head = lambda bi, h: (bi, 0, h)
    blk = 4 * _nbytes((s, HEAD_DIM), F32) + _nbytes((s, LANES), F32) + _nbytes((N_HEAD, s), F32)
    scratch = 3 * _nbytes((s, HEAD_DIM), BF16)
    seq = pl.BlockSpec((1, s, HEAD_DIM), head)
    slabs = []
    for w in to_bf16:
        rows = w.shape[0] // (b * N_HEAD)
        assert rows * b * N_HEAD == w.shape[0], "one slab of rows per grid step"
        slabs.append(pl.BlockSpec((rows, w.shape[1]), lambda bi, h: (bi * N_HEAD + h, 0)))
        blk += _nbytes((rows, w.shape[1]), F32) + _nbytes((rows, w.shape[1]), BF16)
    return pl.pallas_call(
        functools.partial(kernel, tq=tq),
        grid=(b, N_HEAD),
        in_specs=[seq, seq, seq] + extra_specs + slabs,
        out_specs=[seq] + slabs,
        out_shape=[jax.ShapeDtypeStruct(q.shape, F32)]
                  + [jax.ShapeDtypeStruct(w.shape, BF16) for w in to_bf16],
        scratch_shapes=[pltpu.VMEM((s, HEAD_DIM), BF16)] * 3,
        compiler_params=_params(blk, scratch, semantics=("arbitrary", "arbitrary")),
        name=name,
    )(q, k, v, *extra, *to_bf16)


def _fox_prompt(q, k, v, c, c_t, tq=256):
    s = q.shape[1]
    specs = [pl.BlockSpec((1, s, LANES), lambda bi, h: (bi, 0, 0)),
             pl.BlockSpec((1, N_HEAD, s), lambda bi, h: (bi, 0, 0))]
    return _prompt_attention(_fox_prompt_kernel, q, k, v, (c, c_t), specs, tq, "fox_prompt")[0]


def _sb_prompt(q, k, v, to_bf16, tq=256):
    return _prompt_attention(_sb_prompt_kernel, q, k, v, (), [], tq, "sb_prompt", to_bf16)


def _head_rows(page_ref, h, n):
    return page_ref[pl.ds(h, n, stride=N_HEAD), :]


def _stack_heads(rows8):
    t = 8
    return jnp.concatenate(
        [jnp.broadcast_to(rows8[h:h + 1, :], (t, LANES)) for h in range(N_HEAD)], axis=0)


def _decode_scores(q_ref, kn_ref, k_refs, page):
    t = q_ref.shape[1]
    q = q_ref[0]
    first = lax.broadcasted_iota(jnp.int32, (t, 2 * HEAD_DIM), 1) < HEAD_DIM
    qpair = []
    for pr in range(N_HEAD // 2):
        qq = q[:, pr * 2 * HEAD_DIM:(pr + 1) * 2 * HEAD_DIM]
        qpair.append(jnp.concatenate([jnp.where(first, qq, 0.0), jnp.where(first, 0.0, qq)],
                                     axis=0).astype(BF16))

    def scores(pair_keys):
        outs = [lax.dot_general(qpair[pr], pair_keys(pr), NT_DIMS, preferred_element_type=F32)
                for pr in range(N_HEAD // 2)]
        return jnp.concatenate(outs, axis=0) * SCALE

    chunks = []
    for kr in k_refs:
        chunks.append(scores(lambda pr: jnp.concatenate(
            [_head_rows(kr, 2 * pr, page), _head_rows(kr, 2 * pr + 1, page)],
            axis=1).astype(BF16)))
    kn = jnp.concatenate([kn_ref[0], jnp.zeros((page - t, W_MIX), F32)], axis=0).astype(BF16)
    chunks.append(scores(lambda pr: kn[:, pr * 2 * HEAD_DIM:(pr + 1) * 2 * HEAD_DIM]))
    return chunks


def _decode_apply(w_chunks, vn_ref, v_refs, page):
    t = vn_ref.shape[1]
    acc = [jnp.zeros((t, HEAD_DIM), F32) for _ in range(N_HEAD)]

    def apply(w, pair_values):
        for pr in range(N_HEAD // 2):
            o = jnp.dot(w[pr * 2 * t:(pr + 1) * 2 * t, :].astype(BF16), pair_values(pr),
                        preferred_element_type=F32)
            acc[2 * pr] = acc[2 * pr] + o[:t, :HEAD_DIM]
            acc[2 * pr + 1] = acc[2 * pr + 1] + o[t:, HEAD_DIM:]

    for w, vr in zip(w_chunks[:-1], v_refs):
        apply(w, lambda pr: jnp.concatenate(
            [_head_rows(vr, 2 * pr, page), _head_rows(vr, 2 * pr + 1, page)],
            axis=1).astype(BF16))
    vn = jnp.concatenate([vn_ref[0], jnp.zeros((page - t, W_MIX), F32)], axis=0).astype(BF16)
    apply(w_chunks[-1], lambda pr: vn[:, pr * 2 * HEAD_DIM:(pr + 1) * 2 * HEAD_DIM])
    return acc


def _fox_decode(tok_refs, page_refs, o_ref, *, page):
    q_ref, kn_ref, vn_ref, lfn_ref = tok_refs
    k_refs, v_refs, lf_refs = page_refs
    t = q_ref.shape[1]
    rows = N_HEAD * t

    tri3 = _tri_rhs(3, suffix=False)
    carry = jnp.zeros((N_HEAD, 1), F32)
    c_chunks = []
    for lr in lf_refs:
        loc = _prefix_lanes(lr[...], tri3)
        c_chunks.append(loc + carry)
        carry = carry + loc[:, LANES - 1:LANES]
    c_new = _prefix_lanes(lfn_ref[0], tri3) + carry
    c_chunks.append(c_new)

    sub = lax.broadcasted_iota(jnp.int32, (rows, LANES), 0) & (t - 1)
    lane = lax.broadcasted_iota(jnp.int32, (rows, LANES), 1)
    c_new_rows = _stack_heads(c_new)
    cq = jnp.sum(jnp.where(lane == sub, c_new_rows, 0.0), axis=1, keepdims=True)

    s_chunks = _decode_scores(q_ref, kn_ref, k_refs, page)
    s_chunks = [s + cq - _stack_heads(c) for s, c in zip(s_chunks, c_chunks)]
    s_chunks[-1] = jnp.where(lane <= sub, s_chunks[-1], -jnp.inf)

    mx = s_chunks[0]
    for s in s_chunks[1:]:
        mx = jnp.maximum(mx, s)
    m = jnp.max(mx, axis=1, keepdims=True)
    p_chunks = [jnp.exp(s - m) for s in s_chunks]
    tot = p_chunks[0]
    for p in p_chunks[1:]:
        tot = tot + p
    l = jnp.sum(tot, axis=1, keepdims=True)

    acc = _decode_apply(p_chunks, vn_ref, v_refs, page)
    o_ref[0] = jnp.concatenate(
        [acc[h] / l[h * t:(h + 1) * t, :] for h in range(N_HEAD)], axis=1)


def _sb_decode(tok_refs, page_refs, o_ref, *, page):
    q_ref, kn_ref, vn_ref = tok_refs
    k_refs, v_refs = page_refs
    t = q_ref.shape[1]
    rows = N_HEAD * t

    sub = lax.broadcasted_iota(jnp.int32, (rows, LANES), 0) & (t - 1)
    lane = lax.broadcasted_iota(jnp.int32, (rows, LANES), 1)
    strict_new = lane < sub

    z_chunks = _decode_scores(q_ref, kn_ref, k_refs, page)
    log_beta, log_1m = [], []
    for z in z_chunks:
        lb, l1 = _log_sigmoid_pair(z)
        log_beta.append(lb)
        log_1m.append(l1)
    log_1m[-1] = jnp.where(strict_new, log_1m[-1], 0.0)

    tri2 = _tri_rhs(2, suffix=True)
    later = jnp.zeros((rows, 1), F32)
    a_chunks = [None] * len(z_chunks)
    for c in range(len(z_chunks) - 1, -1, -1):
        a = jnp.exp(log_beta[c] + _suffix_lanes(log_1m[c], tri2) + later)
        if c == len(z_chunks) - 1:
            a = jnp.where(strict_new, a, 0.0)
        a_chunks[c] = a
        later = later + jnp.sum(log_1m[c], axis=1, keepdims=True)

    acc = _decode_apply(a_chunks, vn_ref, v_refs, page)
    o_ref[0] = jnp.concatenate(acc, axis=1)


def _mlp_decode_kernel(pt_ref, hn_ref, wu_ref, wd_ref, init_ref, g_ref, *refs, decode, normalize,
                       n_tok, n_cache):
    tok_refs = refs[:n_tok]
    cache_refs = refs[n_tok:n_tok + n_cache]
    y_ref, o_ref = refs[n_tok + n_cache:n_tok + n_cache + 2]
    bufs = refs[n_tok + n_cache + 2:n_tok + 2 * n_cache + 2]
    sem = refs[n_tok + 2 * n_cache + 2]
    n_pages = bufs[0].shape[1]
    n_steps = pl.num_programs(0) * pl.num_programs(1)
    step = pl.program_id(0) * pl.num_programs(1) + pl.program_id(1)
    slot = lax.rem(step, 2)

    def page_copies(seq, to_slot):
        return [pltpu.make_async_copy(cache.at[pt_ref[seq, p]], buf.at[to_slot, p],
                                      sem.at[to_slot, kind])
                for kind, (cache, buf) in enumerate(zip(cache_refs, bufs))
                for p in range(n_pages)]

    @pl.when(step == 0)
    def _():
        for copy in page_copies(0, 0):
            copy.start()

    following = jnp.minimum(step + 1, n_steps - 1)
    for copy in page_copies(following, 1 - slot):
        copy.start(priority=1)
    _mlp_step(hn_ref, wu_ref, wd_ref, init_ref, g_ref, y_ref, normalize=normalize)
    for copy in page_copies(step, slot):
        copy.wait()
    decode(tok_refs, [[buf.at[slot, p] for p in range(n_pages)] for buf in bufs], o_ref)

    @pl.when(step == n_steps - 1)
    def _():
        for copy in page_copies(following, 1 - slot):
            copy.wait()


def _mlp_with_decode(name, hn, w_up_bf, w_down_bf, init, g_final, f_lo, n_f, normalize,
                     page_table, q, kn, vn, cache_k, cache_v, lf_new=None, cache_lf=None,
                     tm=512, tf=512):
    m, d = init.shape
    b, t, _ = q.shape
    n_i = m // tm
    assert n_i * n_f == b, "one decode sequence per MLP grid step"
    n_pages = page_table.shape[1]
    page = cache_k.shape[1] // N_HEAD
    seq = lambda i, f: i * n_f + f
    tok = pl.BlockSpec((1, t, W_MIX), lambda i, f, pt: (seq(i, f), 0, 0))
    in_hbm = pl.BlockSpec(memory_space=pl.ANY)

    in_specs = [pl.BlockSpec((tm, d), lambda i, f, pt: (i, 0)),
                pl.BlockSpec((d, tf), lambda i, f, pt: (0, f_lo + f)),
                pl.BlockSpec((tf, d), lambda i, f, pt: (f_lo + f, 0)),
                pl.BlockSpec((tm, d), lambda i, f, pt: (i, 0), pipeline_mode=pl.Buffered(1)),
                pl.BlockSpec((1, d), lambda i, f, pt: (0, 0)),
                tok, tok, tok]
    args = [hn, w_up_bf, w_down_bf, init, g_final.reshape(1, d), q, kn, vn]
    caches = [cache_k, cache_v]
    decode = _sb_decode
    if cache_lf is not None:
        decode = _fox_decode
        in_specs.append(pl.BlockSpec((1, N_HEAD, LANES), lambda i, f, pt: (seq(i, f), 0, 0)))
        args.append(lf_new)
        caches.append(cache_lf)
    n_tok = len(args) - 5
    page_bufs = [pltpu.VMEM((2, n_pages) + c.shape[1:], F32) for c in caches]
    double = (_nbytes((tm, d), BF16) + _nbytes((d, tf), BF16) + _nbytes((tf, d), BF16)
              + _nbytes((tm, d), F32) + (n_tok + 1) * _nbytes((t, W_MIX), F32))
    single = _nbytes((tm, d), F32) + sum(_nbytes(p.shape, F32) for p in page_bufs)
    return pl.pallas_call(
        functools.partial(_mlp_decode_kernel, normalize=normalize, n_tok=n_tok,
                          n_cache=len(caches), decode=functools.partial(decode, page=page)),
        grid_spec=pltpu.PrefetchScalarGridSpec(
            num_scalar_prefetch=1,
            grid=(n_i, n_f),
            in_specs=in_specs + [in_hbm] * len(caches),
            out_specs=[pl.BlockSpec((tm, d), lambda i, f, pt: (i, 0)),
                       pl.BlockSpec((1, t, W_MIX), lambda i, f, pt: (seq(i, f), 0, 0))],
            scratch_shapes=page_bufs + [pltpu.SemaphoreType.DMA((2, len(caches)))],
        ),
        out_shape=[jax.ShapeDtypeStruct((m, d), F32), jax.ShapeDtypeStruct(q.shape, F32)],
        compiler_params=_params(double, single, semantics=("arbitrary", "arbitrary")),
        name=name,
    )(page_table, *args, *caches)


MLP_TILE = 512


def kernel(x_prompt, x_sample, cache_fox_k, cache_fox_v, cache_fox_logf, cache_sb_k, cache_sb_v,
           page_table, g_mix, w_in, b_f, g_out_fox, g_out_sb, w_out, g_ffn, w_up, w_down, g_final):
    depth = w_in.shape[0]
    assert depth == 1, "single-layer trunk"
    b, s, d = x_prompt.shape
    db, t, _ = x_sample.shape
    n_pool, page = cache_fox_k.shape[1], cache_fox_k.shape[2]
    assert page == LANES and cache_fox_k.shape[3:] == (N_HEAD, HEAD_DIM)

    n_main = N_SEG * W_MIX
    w_in_bf = w_in[0, :, :n_main].astype(BF16)
    wg_bf = jnp.pad(w_in[0, :, n_main:], ((0, 0), (0, LANES - N_HEAD))).astype(BF16)
    bg = jnp.pad(b_f[0], (0, LANES - N_HEAD)).reshape(1, LANES).astype(F32)
    w_out_bf = w_out[0].astype(BF16)
    mix_out = (g_out_fox[0], g_out_sb[0])

    xp = x_prompt.reshape(b * s, d)
    (qf, kf, vf, qs, ks, vs), lf, lf_heads = _project(xp, g_mix[0], w_in_bf, wg_bf, bg)
    xs = x_sample.reshape(db * t, d)
    (qf_s, kf_s, vf_s, qs_s, ks_s, vs_s), lf_s, lf_heads_s = _project(xs, g_mix[0], w_in_bf,
                                                                      wg_bf, bg)

    as_seq = lambda a: a.reshape(b, s, a.shape[-1])
    c, c_t = _cumsum_seq(as_seq(lf))
    o_f = _fox_prompt(as_seq(qf), as_seq(kf), as_seq(vf), c, c_t)
    o_s, w_up_bf, w_down_bf = _sb_prompt(as_seq(qs), as_seq(ks), as_seq(vs),
                                         (w_up[0], w_down[0]))
    h, hn = _attn_out(o_f.reshape(b * s, W_MIX), o_s.reshape(b * s, W_MIX), *mix_out, xp,
                      w_out_bf, g_ffn[0])

    as_tok = lambda a: a.reshape(db, t, a.shape[-1])
    rows_view = lambda cache: cache[0].reshape(n_pool, page * N_HEAD, HEAD_DIM)
    lf_cache_t = jnp.swapaxes(cache_fox_logf[0], 1, 2)
    lf_new_t = jnp.swapaxes(lf_heads_s.reshape(db, t, N_HEAD), 1, 2)
    lf_new_t = jnp.pad(lf_new_t, ((0, 0), (0, 0), (0, LANES - t)))
    n_f = db // (b * s // MLP_TILE)
    assert 2 * n_f * MLP_TILE == w_up_bf.shape[1], "two halves cover the hidden width"
    mlp_in = (hn, w_up_bf, w_down_bf)
    y_half, o_f_s = _mlp_with_decode(
        "mlp_fox_decode", *mlp_in, h, g_final, 0, n_f, False, page_table,
        as_tok(qf_s), as_tok(kf_s), as_tok(vf_s), rows_view(cache_fox_k),
        rows_view(cache_fox_v), lf_new=lf_new_t, cache_lf=lf_cache_t,
        tm=MLP_TILE, tf=MLP_TILE)
    y_prompt, o_s_s = _mlp_with_decode(
        "mlp_sb_decode", *mlp_in, y_half, g_final, n_f, n_f, True, page_table,
        as_tok(qs_s), as_tok(ks_s), as_tok(vs_s), rows_view(cache_sb_k),
        rows_view(cache_sb_v), tm=MLP_TILE, tf=MLP_TILE)

    h_s, hn_s = _attn_out(o_f_s.reshape(db * t, W_MIX), o_s_s.reshape(db * t, W_MIX), *mix_out,
                          xs, w_out_bf, g_ffn[0])
    y_sample = _mlp(hn_s, w_up_bf, w_down_bf, h_s, g_final)

    heads = lambda a, n0, n1: a.reshape(1, n0, n1, N_HEAD, HEAD_DIM)
    return (y_prompt.reshape(b, s, d), y_sample.reshape(db, t, d),
            heads(kf, b, s), heads(vf, b, s), lf_heads.reshape(1, b, s, N_HEAD),
            heads(ks, b, s), heads(vs, b, s),
            heads(kf_s, db, t), heads(vf_s, db, t), lf_heads_s.reshape(1, db, t, N_HEAD),
            heads(ks_s, db, t), heads(vs_s, db, t))
```
